```python
import math
import jax, jax.numpy as jnp
from jax import lax
import numpy as np

D_MODEL = 1024
BATCH = 8
SEQ = 4096
DEPTH = 4

N_META = 16
N_MIXERS = 2
N_ATTN_LAYERS = (DEPTH + 1) // 2
N_GLA_LAYERS = DEPTH // 2
DA_HEADS = 8
DA_HEAD_DIM = D_MODEL // DA_HEADS // 2
DA_V_DIM = 2 * DA_HEAD_DIM
Q_BLOCK = 128
GLA_HEADS = 4
GLA_KEY_DIM = D_MODEL // 2
GLA_VAL_DIM = D_MODEL
GLA_HK = GLA_KEY_DIM // GLA_HEADS
GLA_HV = GLA_VAL_DIM // GLA_HEADS
GLA_GATE_RANK = 16
GLA_GATE_NORM = 16.0
GLA_CHUNK = 64
GLA_IN_DIM = 2 * GLA_KEY_DIM + 2 * GLA_VAL_DIM + GLA_GATE_RANK
D_FF = 4 * D_MODEL
EPS = 1e-6

kernel_name = "hybrid_diffattn_gla_sqrelu"


def rmsnorm(x, w):
    xf = x.astype(jnp.float32)
    y = xf * lax.rsqrt(jnp.mean(xf * xf, axis=-1, keepdims=True) + EPS)
    return (y * w.astype(jnp.float32)).astype(x.dtype)


def lambda_init_for(layer_idx):
    return 0.8 - 0.6 * math.exp(-0.3 * layer_idx)


def alibi_slopes(n_heads):
    return 2.0 ** (-8.0 * jnp.arange(1, n_heads + 1, dtype=jnp.float32) / n_heads)


def diff_attention(x, w_in, lam_params, subln_w, w_out, lambda_init):
    B, L, _ = x.shape
    q, k, v = jnp.split(x @ w_in, [D_MODEL, 2 * D_MODEL], axis=-1)
    q = q.reshape(B, L, DA_HEADS, 2, DA_HEAD_DIM)
    k = k.reshape(B, L, DA_HEADS, 2, DA_HEAD_DIM)
    v = v.reshape(B, L, DA_HEADS, DA_V_DIM)
    lp = lam_params.astype(jnp.float32)
    lam = jnp.exp(jnp.sum(lp[0] * lp[1])) - jnp.exp(jnp.sum(lp[2] * lp[3])) + lambda_init
    slopes = alibi_slopes(DA_HEADS)[None, :, None, None, None]
    scale = DA_HEAD_DIM ** -0.5
    bounds = [(0, N_META)] + [(s, min(s + Q_BLOCK, L)) for s in range(N_META, L, Q_BLOCK)]
    outs = []
    for s, e in bounds:
        qb = q[:, s:e]
        kb = k[:, :e]
        vb = v[:, :e]
        scores = jnp.einsum('bqhcd,bkhcd->bhcqk', qb, kb).astype(jnp.float32) * scale
        dist = (jnp.arange(s, e)[:, None] - jnp.arange(e)[None, :]).astype(jnp.float32)
        scores = jnp.where(dist >= 0.0, scores - slopes * dist, -jnp.inf)
        p = jax.nn.softmax(scores, axis=-1)
        attn = p[:, :, 0] - lam * p[:, :, 1]
        outs.append(jnp.einsum('bhqk,bkhe->bqhe', attn.astype(vb.dtype), vb))
    o = jnp.concatenate(outs, axis=1)
    o = rmsnorm(o, subln_w) * (1.0 - lambda_init)
    return o.reshape(B, L, DA_HEADS * DA_V_DIM) @ w_out


def gla_chunk(S, q, k, v, glog):
    C = q.shape[2]
    b = jnp.cumsum(glog, axis=2)
    causal = jnp.tril(jnp.ones((C, C), dtype=bool))
    diff = b[:, :, :, None, :] - b[:, :, None, :, :]
    decay = jnp.exp(jnp.where(causal[:, :, None], diff, -jnp.inf))
    a = jnp.einsum('bhid,bhjd,bhijd->bhij', q, k, decay)
    o = jnp.einsum('bhij,bhje->bhie', a, v) + jnp.einsum('bhid,bhde->bhie', q * jnp.exp(b), S)
    b_last = b[:, :, -1:, :]
    S = jnp.exp(b_last[:, :, 0, :, None]) * S + jnp.einsum('bhjd,bhje->bhde', k * jnp.exp(b_last - b), v)
    return S, o


def gla(x, w_in, w_gate_up, gate_bias, norm_w, w_out):
    B, L, _ = x.shape
    splits = [GLA_KEY_DIM, 2 * GLA_KEY_DIM, 2 * GLA_KEY_DIM + GLA_VAL_DIM, 2 * GLA_KEY_DIM + 2 * GLA_VAL_DIM]
    q, k, v, g, gz = jnp.split(x @ w_in, splits, axis=-1)
    glog = jax.nn.log_sigmoid((gz @ w_gate_up + gate_bias).astype(jnp.float32)) / GLA_GATE_NORM

    def heads(t, d):
        return t.reshape(B, L, GLA_HEADS, d).transpose(0, 2, 1, 3).astype(jnp.float32)

    q = heads(q, GLA_HK) * (GLA_HK ** -0.5)
    k = heads(k, GLA_HK)
    v = heads(v, GLA_HV)
    glog = heads(glog, GLA_HK)
    S0 = jnp.zeros((B, GLA_HEADS, GLA_HK, GLA_HV), jnp.float32)
    S, o_meta = gla_chunk(S0, q[:, :, :N_META], k[:, :, :N_META], v[:, :, :N_META], glog[:, :, :N_META])
    n_real = L - N_META
    n_chunks = n_real // GLA_CHUNK

    def to_chunks(t):
        t = t[:, :, N_META:]
        return t.reshape(B, GLA_HEADS, n_chunks, GLA_CHUNK, t.shape[-1]).transpose(2, 0, 1, 3, 4)

    def step(state, inp):
        return gla_chunk(state, *inp)

    _, o_real = lax.scan(step, S, (to_chunks(q), to_chunks(k), to_chunks(v), to_chunks(glog)))
    o_real = o_real.transpose(1, 2, 0, 3, 4).reshape(B, GLA_HEADS, n_real, GLA_HV)
    o = jnp.concatenate([o_meta, o_real], axis=2).transpose(0, 2, 1, 3)
    o = rmsnorm(o, norm_w) * jax.nn.silu(g.reshape(B, L, GLA_HEADS, GLA_HV).astype(jnp.float32))
    return o.reshape(B, L, GLA_VAL_DIM).astype(x.dtype) @ w_out


def sq_relu_mlp(x, w_up, w_down):
    return jnp.square(jax.nn.relu(x @ w_up)) @ w_down


def setup_inputs(seed: int = 0) -> dict:
    key = jax.random.key(seed)
    ks = jax.random.split(key, 17)

    def nrm(k, shape, scale):
        return jax.random.normal(k, shape, jnp.float32) * scale

    return {
        "x": nrm(ks[0], (BATCH, SEQ, D_MODEL), 1.0),
        "meta_tokens": nrm(ks[1], (N_META, D_MODEL), 1.0),
        "mix_norm_w": 1.0 + nrm(ks[2], (DEPTH, D_MODEL), 0.02),
        "attn_w_in": nrm(ks[3], (N_ATTN_LAYERS, D_MODEL, 3 * D_MODEL), D_MODEL ** -0.5),
        "attn_lambda": nrm(ks[4], (N_ATTN_LAYERS, 4, DA_HEAD_DIM), 0.1),
        "attn_subln_w": 1.0 + nrm(ks[5], (N_ATTN_LAYERS, DA_V_DIM), 0.02),
        "attn_w_out": nrm(ks[6], (N_ATTN_LAYERS, DA_HEADS * DA_V_DIM, D_MODEL), (DA_HEADS * DA_V_DIM) ** -0.5),
        "gla_w_in": nrm(ks[7], (N_GLA_LAYERS, D_MODEL, GLA_IN_DIM), D_MODEL ** -0.5),
        "gla_w_gate_up": nrm(ks[8], (N_GLA_LAYERS, GLA_GATE_RANK, GLA_KEY_DIM), GLA_GATE_RANK ** -0.5),
        "gla_gate_bias": nrm(ks[9], (N_GLA_LAYERS, GLA_KEY_DIM), 0.02),
        "gla_norm_w": 1.0 + nrm(ks[10], (N_GLA_LAYERS, GLA_HV), 0.02),
        "gla_w_out": nrm(ks[11], (N_GLA_LAYERS, GLA_VAL_DIM, D_MODEL), GLA_VAL_DIM ** -0.5),
        "mlp_norm_w": 1.0 + nrm(ks[12], (DEPTH, D_MODEL), 0.02),
        "mlp_w_up": nrm(ks[13], (DEPTH, D_MODEL, D_FF), D_MODEL ** -0.5),
        "mlp_w_down": nrm(ks[14], (DEPTH, D_FF, D_MODEL), D_FF ** -0.5),
        "final_norm_w": 1.0 + nrm(ks[15], (D_MODEL,), 0.02),
    }


def reference(x, meta_tokens, mix_norm_w, attn_w_in, attn_lambda, attn_subln_w, attn_w_out,
              gla_w_in, gla_w_gate_up, gla_gate_bias, gla_norm_w, gla_w_out,
              mlp_norm_w, mlp_w_up, mlp_w_down, final_norm_w):
    B = x.shape[0]
    meta = jnp.broadcast_to(meta_tokens[None].astype(x.dtype), (B, N_META, D_MODEL))
    h = jnp.concatenate([meta, x], axis=1)
    for i in range(DEPTH):
        hn = rmsnorm(h, mix_norm_w[i])
        j = i // N_MIXERS
        if i % N_MIXERS == 0:
            h = h + diff_attention(hn, attn_w_in[j], attn_lambda[j], attn_subln_w[j], attn_w_out[j],
                                   lambda_init_for(i))
        else:
            h = h + gla(hn, gla_w_in[j], gla_w_gate_up[j], gla_gate_bias[j], gla_norm_w[j], gla_w_out[j])
        h = h + sq_relu_mlp(rmsnorm(h, mlp_norm_w[i]), mlp_w_up[i], mlp_w_down[i])
    return rmsnorm(h[:, N_META:], final_norm_w)
```

```python
import functools
import math

import numpy as np
import jax
import jax.numpy as jnp
from jax import lax
from jax.experimental import pallas as pl
from jax.experimental.pallas import tpu as pltpu

F32 = jnp.float32
BF16 = jnp.bfloat16

D_MODEL = 1024
DEPTH = 4
N_META = 16
PAD = 128
FIRST_KEY = PAD - N_META
EPS = 1e-6
NEG = -1e30

DA_HEADS = 8
DA_HEAD_DIM = 64
DA_V_DIM = 128
GLA_HEADS = 4
GLA_HK = 128
GLA_HV = 256
GLA_KEY_DIM = 512
GLA_VAL_DIM = 1024
GLA_GATE_RANK = 16
GLA_GATE_NORM = 16.0
GLA_CHUNK = 128
GLA_LEVELS = 7
D_FF = 4096

LANE = 128
VMEM_LIMIT = 50 * 1024 * 1024

NT_DIMS = (((1,), (1,)), ((), ()))
TN_DIMS = (((0,), (0,)), ((), ()))


def _row_tile(n_rows):
    for t in (512, 384, 256, 128):
        if n_rows % t == 0:
            return t
    raise ValueError(f"row count {n_rows} is not a multiple of 128")


def _rms_scale(x, w):
    return x * lax.rsqrt(jnp.mean(x * x, axis=-1, keepdims=True) + EPS) * w


def _resident(shape):
    return pl.BlockSpec(shape, lambda *_: (0,) * len(shape), pipeline_mode=pl.Buffered(1))


def _norm_proj_body(h_ref, nw_ref, w_ref, o_ref, *, n_chunk):
    hn = _rms_scale(h_ref[...], nw_ref[...]).astype(BF16)
    for c in range(0, w_ref.shape[1], n_chunk):
        o_ref[:, c:c + n_chunk] = jnp.dot(
            hn, w_ref[:, c:c + n_chunk], preferred_element_type=F32).astype(o_ref.dtype)


def norm_proj(h, norm_w, w):
    T, D = h.shape
    N = w.shape[1]
    tm = _row_tile(T)
    return pl.pallas_call(
        functools.partial(_norm_proj_body, n_chunk=512),
        grid=(T // tm,),
        in_specs=[pl.BlockSpec((tm, D), lambda i: (i, 0)),
                  _resident((1, D)),
                  _resident((D, N))],
        out_specs=pl.BlockSpec((tm, N), lambda i: (i, 0)),
        out_shape=jax.ShapeDtypeStruct((T, N), BF16),
        compiler_params=pltpu.CompilerParams(
            dimension_semantics=("parallel",), vmem_limit_bytes=VMEM_LIMIT),
        name="norm_proj",
    )(h, norm_w.reshape(1, D), w)


def _gla_proj_body(h_ref, nw_ref, w_ref, wgz_ref, wgu_ref, gb_ref, o_ref, gl_ref, *, n_chunk):
    hn = _rms_scale(h_ref[...], nw_ref[...]).astype(BF16)
    for c in range(0, w_ref.shape[1], n_chunk):
        o_ref[:, c:c + n_chunk] = jnp.dot(
            hn, w_ref[:, c:c + n_chunk], preferred_element_type=F32).astype(o_ref.dtype)
    gz = jnp.dot(hn, wgz_ref[...], preferred_element_type=F32).astype(BF16)
    z = jnp.dot(gz, wgu_ref[...], preferred_element_type=F32) + gb_ref[...]
    gl_ref[...] = (jnp.minimum(z, 0.0) - jnp.log1p(jnp.exp(-jnp.abs(z)))) * (1.0 / GLA_GATE_NORM)


def gla_proj(h, norm_w, w_main, w_gz, w_gate_up, gate_bias):
    T, D = h.shape
    N = w_main.shape[1]
    tm = _row_tile(T)
    return pl.pallas_call(
        functools.partial(_gla_proj_body, n_chunk=512),
        grid=(T // tm,),
        in_specs=[pl.BlockSpec((tm, D), lambda i: (i, 0)),
                  _resident((1, D)),
                  _resident((D, N)),
                  _resident((D, LANE)),
                  _resident((LANE, GLA_KEY_DIM)),
                  _resident((1, GLA_KEY_DIM))],
        out_specs=[pl.BlockSpec((tm, N), lambda i: (i, 0)),
                   pl.BlockSpec((tm, GLA_KEY_DIM), lambda i: (i, 0))],
        out_shape=[jax.ShapeDtypeStruct((T, N), BF16),
                   jax.ShapeDtypeStruct((T, GLA_KEY_DIM), F32)],
        compiler_params=pltpu.CompilerParams(
            dimension_semantics=("parallel",), vmem_limit_bytes=VMEM_LIMIT),
        name="gla_proj",
    )(h, norm_w.reshape(1, D), w_main, w_gz, w_gate_up, gate_bias.reshape(1, GLA_KEY_DIM))


def _attn_body(slopes_ref, q_ref, k_ref, v_ref, lam_ref, sub_ref, o_ref, *, tq, tk, seq_rows, lambda_init):
    h = pl.program_id(1)
    qi = pl.program_id(2)
    slope = slopes_ref[h]
    row0 = qi * tq

    q = q_ref[...] * (DA_HEAD_DIM ** -0.5)
    lane = lax.broadcasted_iota(jnp.int32, q.shape, 1)
    zero = jnp.zeros_like(q)
    qs = jnp.concatenate([jnp.where(lane < DA_HEAD_DIM, q, zero),
                          jnp.where(lane >= DA_HEAD_DIM, q, zero)], axis=0)

    def tile(start, carry, lo=None):
        m, l, acc = carry
        k = k_ref[pl.ds(start, tk), :]
        v = v_ref[pl.ds(start, tk), :]
        s = lax.dot_general(qs, k, NT_DIMS, preferred_element_type=F32)
        col = start + lax.broadcasted_iota(jnp.int32, (1, tk), 1)
        bias = jnp.where(col >= FIRST_KEY, slope * (col - row0).astype(F32), NEG)
        s = s + bias
        if lo is not None:
            r = lax.broadcasted_iota(jnp.int32, (2 * tq, 1), 0)
            row = row0 + jnp.where(r >= tq, r - tq, r)
            s = jnp.where((col <= row) & (col >= lo), s, NEG)
        m_new = jnp.maximum(m, jnp.max(s, axis=1, keepdims=True))
        alpha = jnp.exp(m - m_new)
        p = jnp.exp(s - m_new)
        l = alpha * l + jnp.sum(p, axis=1, keepdims=True)
        acc = alpha * acc + jnp.dot(p.astype(BF16), v, preferred_element_type=F32)
        return m_new, l, acc

    init = (jnp.full((2 * tq, 1), NEG, F32), jnp.zeros((2 * tq, 1), F32),
            jnp.zeros((2 * tq, DA_V_DIM), F32))
    n_full = row0 // tk
    carry = lax.fori_loop(0, n_full, lambda j, c: tile(pl.multiple_of(j * tk, tk), c), init)
    lo = n_full * tk
    start = pl.multiple_of(jnp.minimum(lo, seq_rows - tk), LANE)
    m, l, acc = tile(start, carry, lo=lo)

    o = acc / l
    lp = lam_ref[...]
    lam = (jnp.exp(jnp.sum(lp[0:1] * lp[1:2], axis=1, keepdims=True))
           - jnp.exp(jnp.sum(lp[2:3] * lp[3:4], axis=1, keepdims=True)) + lambda_init)
    d = o[:tq] - lam * o[tq:]
    row = row0 + lax.broadcasted_iota(jnp.int32, (tq, 1), 0)
    d = jnp.where(row >= FIRST_KEY, d, 0.0)
    y = _rms_scale(d, sub_ref[...]) * (1.0 - lambda_init)
    o_ref[...] = y.astype(o_ref.dtype)


def diff_attention(qkv, lam_params, subln_w, lambda_init):
    B, Lp, _ = qkv.shape
    tq = 128
    tk = 512 if Lp >= 512 else Lp
    slopes = 2.0 ** (-8.0 * jnp.arange(1, DA_HEADS + 1, dtype=F32) / DA_HEADS)
    kv_spec = lambda off: pl.BlockSpec((None, Lp, DA_V_DIM), lambda b, h, i: (b, 0, off + h))
    return pl.pallas_call(
        functools.partial(_attn_body, tq=tq, tk=tk, seq_rows=Lp, lambda_init=lambda_init),
        grid=(B, DA_HEADS, Lp // tq),
        in_specs=[pl.BlockSpec(memory_space=pltpu.SMEM),
                  pl.BlockSpec((None, tq, DA_V_DIM), lambda b, h, i: (b, i, h)),
                  kv_spec(DA_HEADS),
                  kv_spec(2 * DA_HEADS),
                  _resident((4, DA_HEAD_DIM)),
                  _resident((1, DA_V_DIM))],
        out_specs=pl.BlockSpec((None, tq, DA_V_DIM), lambda b, h, i: (b, i, h)),
        out_shape=jax.ShapeDtypeStruct((B, Lp, DA_HEADS * DA_V_DIM), BF16),
        compiler_params=pltpu.CompilerParams(
            dimension_semantics=("parallel", "parallel", "arbitrary"), vmem_limit_bytes=VMEM_LIMIT),
        name="diff_attention",
    )(slopes, qkv, qkv, qkv, lam_params.astype(F32), subln_w.reshape(1, DA_V_DIM).astype(F32))


def _gla_tables():
    C = GLA_CHUNK
    i = np.arange(C)[:, None]
    t = np.arange(C)[None, :]
    blocks = [t <= i, t > i]
    level = np.full((C, C), GLA_LEVELS + 1, np.int32)
    level[np.arange(C), np.arange(C)] = GLA_LEVELS
    for l in range(GLA_LEVELS):
        s = C >> (l + 1)
        m = (i // (2 * s)) * 2 * s + s - 1
        upper = (i & s) != 0
        blocks.append(np.where(upper, (t > m) & (t <= i), (t > i) & (t <= m)))
        level[(i > t) & (((i ^ t) >> (GLA_LEVELS - 1 - l)) == 1)] = l
    sums = np.concatenate(blocks, axis=0).astype(np.float32)
    return jnp.asarray(sums, BF16), jnp.asarray(level)


def _gla_body(level_ref, sums_ref, q_ref, k_ref, v_ref, g_ref, gl_ref, nw_ref, o_ref, st_ref):
    C = GLA_CHUNK

    @pl.when(pl.program_id(1) == 0)
    def _():
        st_ref[...] = jnp.zeros_like(st_ref)

    gl = gl_ref[...]
    g_hi = gl.astype(BF16)
    rem = gl - g_hi.astype(F32)
    g_mid = rem.astype(BF16)
    g_lo = (rem - g_mid.astype(F32)).astype(BF16)
    sums = sums_ref[...]
    w_all = (jnp.dot(sums, g_hi, preferred_element_type=F32)
             + jnp.dot(sums, g_mid, preferred_element_type=F32)
             + jnp.dot(sums, g_lo, preferred_element_type=F32))
    level = level_ref[...]

    for h in range(GLA_HEADS):
        ks = slice(h * GLA_HK, (h + 1) * GLA_HK)
        vs = slice(h * GLA_HV, (h + 1) * GLA_HV)
        q = q_ref[:, ks].astype(F32) * (GLA_HK ** -0.5)
        k = k_ref[:, ks].astype(F32)
        v = v_ref[:, vs]
        w = w_all[:, ks]
        cum = w[0:C]
        a = jnp.where(level == GLA_LEVELS, jnp.sum(q * k, axis=1, keepdims=True), 0.0)
        for l in range(GLA_LEVELS):
            e = jnp.exp(w[(2 + l) * C:(3 + l) * C])
            a_l = lax.dot_general((q * e).astype(BF16), (k * e).astype(BF16), NT_DIMS,
                                  preferred_element_type=F32)
            a = jnp.where(level == l, a_l, a)
        state_t = st_ref[h]
        qe = (q * jnp.exp(cum)).astype(BF16)
        o = (jnp.dot(a.astype(BF16), v, preferred_element_type=F32)
             + lax.dot_general(qe, state_t.astype(BF16), NT_DIMS, preferred_element_type=F32))
        kd = (k * jnp.exp(w[C:2 * C])).astype(BF16)
        st_ref[h] = (state_t * jnp.exp(cum[C - 1:C])
                     + lax.dot_general(v, kd, TN_DIMS, preferred_element_type=F32))
        g = g_ref[:, vs].astype(F32)
        y = _rms_scale(o, nw_ref[...]) * (g / (1.0 + jnp.exp(-g)))
        o_ref[:, vs] = y.astype(o_ref.dtype)


def gla_mixer(qkvg, glog, norm_w):
    B, Lp, _ = qkvg.shape
    C = GLA_CHUNK
    sums, level = _gla_tables()
    kw, vw = GLA_KEY_DIM, GLA_VAL_DIM
    return pl.pallas_call(
        _gla_body,
        grid=(B, Lp // C),
        in_specs=[_resident((C, C)),
                  _resident(((2 + GLA_LEVELS) * C, C)),
                  pl.BlockSpec((None, C, kw), lambda b, c: (b, c, 0)),
                  pl.BlockSpec((None, C, kw), lambda b, c: (b, c, 1)),
                  pl.BlockSpec((None, C, vw), lambda b, c: (b, c, 1)),
                  pl.BlockSpec((None, C, vw), lambda b, c: (b, c, 2)),
                  pl.BlockSpec((None, C, kw), lambda b, c: (b, c, 0)),
                  _resident((1, GLA_HV))],
        out_specs=pl.BlockSpec((None, C, vw), lambda b, c: (b, c, 0)),
        out_shape=jax.ShapeDtypeStruct((B, Lp, vw), BF16),
        scratch_shapes=[pltpu.VMEM((GLA_HEADS, GLA_HV, GLA_HK), F32)],
        compiler_params=pltpu.CompilerParams(
            dimension_semantics=("parallel", "arbitrary"), vmem_limit_bytes=VMEM_LIMIT),
        name="gla_mixer",
    )(level, sums, qkvg, qkvg, qkvg, qkvg, glog, norm_w.reshape(1, GLA_HV).astype(F32))


def _out_mlp_body(h_ref, o_ref, wo_ref, nw_ref, wu_ref, wd_ref, out_ref, *, f_chunk):
    h1 = h_ref[...] + jnp.dot(o_ref[...], wo_ref[...], preferred_element_type=F32)
    hn = _rms_scale(h1, nw_ref[...]).astype(BF16)
    out_ref[...] = h1
    for c in range(0, wu_ref.shape[1], f_chunk):
        u = jnp.dot(hn, wu_ref[:, c:c + f_chunk], preferred_element_type=F32)
        u = jnp.square(jnp.maximum(u, 0.0)).astype(BF16)
        out_ref[...] += jnp.dot(u, wd_ref[c:c + f_chunk, :], preferred_element_type=F32)


def out_mlp(h, o, w_out, norm_w, w_up, w_down):
    T, D = h.shape
    F = w_up.shape[1]
    tm = _row_tile(T)
    return pl.pallas_call(
        functools.partial(_out_mlp_body, f_chunk=1024),
        grid=(T // tm,),
        in_specs=[pl.BlockSpec((tm, D), lambda i: (i, 0)),
                  pl.BlockSpec((tm, D), lambda i: (i, 0)),
                  _resident((D, D)),
                  _resident((1, D)),
                  _resident((D, F)),
                  _resident((F, D))],
        out_specs=pl.BlockSpec((tm, D), lambda i: (i, 0)),
        out_shape=jax.ShapeDtypeStruct((T, D), F32),
        compiler_params=pltpu.CompilerParams(
            dimension_semantics=("parallel",), vmem_limit_bytes=VMEM_LIMIT),
        name="out_mlp",
    )(h, o, w_out, norm_w.reshape(1, D), w_up, w_down)


def _final_norm_body(h_ref, nw_ref, o_ref):
    o_ref[...] = _rms_scale(h_ref[...], nw_ref[...])


def final_norm(h, norm_w, seq):
    B, Lp, D = h.shape
    t = PAD
    return pl.pallas_call(
        _final_norm_body,
        grid=(B, seq // t),
        in_specs=[pl.BlockSpec((None, t, D), lambda b, i: (b, i + 1, 0)),
                  _resident((1, D))],
        out_specs=pl.BlockSpec((None, t, D), lambda b, i: (b, i, 0)),
        out_shape=jax.ShapeDtypeStruct((B, seq, D), F32),
        compiler_params=pltpu.CompilerParams(dimension_semantics=("parallel", "parallel")),
        name="final_norm",
    )(h, norm_w.reshape(1, D))


def _lambda_init_for(layer_idx):
    return 0.8 - 0.6 * math.exp(-0.3 * layer_idx)


def kernel(x, meta_tokens, mix_norm_w, attn_w_in, attn_lambda, attn_subln_w, attn_w_out, gla_w_in, gla_w_gate_up, gla_gate_bias, gla_norm_w, gla_w_out, mlp_norm_w, mlp_w_up, mlp_w_down, final_norm_w):
    B, seq, D = x.shape
    Lp = PAD + seq
    T = B * Lp
    lead = jnp.concatenate([jnp.zeros((FIRST_KEY, D), x.dtype), meta_tokens.astype(x.dtype)], axis=0)
    h = jnp.concatenate([jnp.broadcast_to(lead[None], (B, PAD, D)), x], axis=1).reshape(T, D)

    for i in range(DEPTH):
        j = i // 2
        if i % 2 == 0:
            qkv = norm_proj(h, mix_norm_w[i], attn_w_in[j].astype(BF16))
            o = diff_attention(qkv.reshape(B, Lp, 3 * D), attn_lambda[j], attn_subln_w[j],
                               _lambda_init_for(i))
            w_out = attn_w_out[j]
        else:
            n_main = 2 * GLA_KEY_DIM + 2 * GLA_VAL_DIM
            w_in = gla_w_in[j]
            w_gz = jnp.pad(w_in[:, n_main:], ((0, 0), (0, LANE - GLA_GATE_RANK))).astype(BF16)
            w_gu = jnp.pad(gla_w_gate_up[j], ((0, LANE - GLA_GATE_RANK), (0, 0))).astype(BF16)
            qkvg, glog = gla_proj(h, mix_norm_w[i], w_in[:, :n_main].astype(BF16), w_gz, w_gu,
                                  gla_gate_bias[j])
            o = gla_mixer(qkvg.reshape(B, Lp, n_main), glog.reshape(B, Lp, GLA_KEY_DIM), gla_norm_w[j])
            w_out = gla_w_out[j]
        h = out_mlp(h, o.reshape(T, D), w_out.astype(BF16), mlp_norm_w[i],
                    mlp_w_up[i].astype(BF16), mlp_w_down[i].astype(BF16))
    return final_norm(h.reshape(B, Lp, D), final_norm_w, seq)
```

```python
import functools
import math

import numpy as np
import jax
import jax.numpy as jnp
from jax import lax
from jax.experimental import pallas as pl
from jax.experimental.pallas import tpu as pltpu

F32 = jnp.float32
BF16 = jnp.bfloat16

D_MODEL = 1024
DEPTH = 4
N_META = 16
PAD = 256
FIRST_KEY = PAD - N_META
EPS = 1e-6
NEG = -1e30

DA_HEADS = 8
DA_HEAD_DIM = 64
DA_V_DIM = 128
GLA_HEADS = 4
GLA_HK = 128
GLA_HV = 256
GLA_KEY_DIM = 512
GLA_VAL_DIM = 1024
GLA_GATE_RANK = 16
GLA_GATE_NORM = 16.0
GLA_CHUNK = 128
GLA_LEVELS = 7
D_FF = 4096

LANE = 128
VMEM_LIMIT = 50 * 1024 * 1024

NT_DIMS = (((1,), (1,)), ((), ()))
TN_DIMS = (((0,), (0,)), ((), ()))


def _row_tile(n_rows):
    for t in (512, 384, 256, 128):
        if n_rows % t == 0:
            return t
    raise ValueError(f"row count {n_rows} is not a multiple of 128")


def _rms_scale(x, w):
    return x * lax.rsqrt(jnp.mean(x * x, axis=-1, keepdims=True) + EPS) * w


def _resident(shape):
    return pl.BlockSpec(shape, lambda *_: (0,) * len(shape), pipeline_mode=pl.Buffered(1))


def _norm_proj_body(h_ref, nw_ref, w_ref, o_ref, *, n_chunk):
    hn = _rms_scale(h_ref[...], nw_ref[...]).astype(BF16)
    for c in range(0, w_ref.shape[1], n_chunk):
        o_ref[:, c:c + n_chunk] = jnp.dot(
            hn, w_ref[:, c:c + n_chunk], preferred_element_type=F32).astype(o_ref.dtype)


def norm_proj(h, norm_w, w):
    T, D = h.shape
    N = w.shape[1]
    tm = _row_tile(T)
    return pl.pallas_call(
        functools.partial(_norm_proj_body, n_chunk=512),
        grid=(T // tm,),
        in_specs=[pl.BlockSpec((tm, D), lambda i: (i, 0)),
                  _resident((1, D)),
                  _resident((D, N))],
        out_specs=pl.BlockSpec((tm, N), lambda i: (i, 0)),
        out_shape=jax.ShapeDtypeStruct((T, N), BF16),
        compiler_params=pltpu.CompilerParams(
            dimension_semantics=("parallel",), vmem_limit_bytes=VMEM_LIMIT),
        name="norm_proj",
    )(h, norm_w.reshape(1, D), w)


def _gla_proj_body(h_ref, nw_ref, w_ref, wgz_ref, wgu_ref, gb_ref, o_ref, gl_ref, *, n_chunk):
    hn = _rms_scale(h_ref[...], nw_ref[...]).astype(BF16)
    for c in range(0, w_ref.shape[1], n_chunk):
        o_ref[:, c:c + n_chunk] = jnp.dot(
            hn, w_ref[:, c:c + n_chunk], preferred_element_type=F32).astype(o_ref.dtype)
    gz = jnp.dot(hn, wgz_ref[...], preferred_element_type=F32).astype(BF16)
    z = jnp.dot(gz, wgu_ref[...], preferred_element_type=F32) + gb_ref[...]
    gl_ref[...] = (jnp.minimum(z, 0.0) - jnp.log1p(jnp.exp(-jnp.abs(z)))) * (1.0 / GLA_GATE_NORM)


def gla_proj(h, norm_w, w_main, w_gz, w_gate_up, gate_bias):
    T, D = h.shape
    N = w_main.shape[1]
    tm = _row_tile(T)
    return pl.pallas_call(
        functools.partial(_gla_proj_body, n_chunk=512),
        grid=(T // tm,),
        in_specs=[pl.BlockSpec((tm, D), lambda i: (i, 0)),
                  _resident((1, D)),
                  _resident((D, N)),
                  _resident((D, LANE)),
                  _resident((LANE, GLA_KEY_DIM)),
                  _resident((1, GLA_KEY_DIM))],
        out_specs=[pl.BlockSpec((tm, N), lambda i: (i, 0)),
                   pl.BlockSpec((tm, GLA_KEY_DIM), lambda i: (i, 0))],
        out_shape=[jax.ShapeDtypeStruct((T, N), BF16),
                   jax.ShapeDtypeStruct((T, GLA_KEY_DIM), F32)],
        compiler_params=pltpu.CompilerParams(
            dimension_semantics=("parallel",), vmem_limit_bytes=VMEM_LIMIT),
        name="gla_proj",
    )(h, norm_w.reshape(1, D), w_main, w_gz, w_gate_up, gate_bias.reshape(1, GLA_KEY_DIM))


ATTN_TQ = 256
ATTN_HEADS_PER_STEP = 2


def _attn_body(slopes_ref, q_ref, k_ref, v_ref, lam_ref, sub_ref, o_ref, *, lambda_init):
    tq = ATTN_TQ
    hp = pl.program_id(1)
    qi = pl.program_id(2)
    row0 = pl.multiple_of(qi * tq, tq)
    lane = lax.broadcasted_iota(jnp.int32, (tq, LANE), 1)
    sub = lax.broadcasted_iota(jnp.int32, (tq, LANE), 0)
    pos = jnp.where(lane == 0, sub, 0).astype(BF16)

    heads = []
    for hh in range(ATTN_HEADS_PER_STEP):
        cs = slice(hh * DA_V_DIM, (hh + 1) * DA_V_DIM)
        slope = slopes_ref[ATTN_HEADS_PER_STEP * hp + hh]
        q = q_ref[:, cs].astype(F32) * (DA_HEAD_DIM ** -0.5)
        feat = jnp.where(lane == 0, slope, 0.0)
        q_aug = jnp.concatenate(
            [jnp.concatenate([jnp.where(lane < DA_HEAD_DIM, q, 0.0), feat], axis=1),
             jnp.concatenate([jnp.where(lane >= DA_HEAD_DIM, q, 0.0), feat], axis=1)], axis=0)
        heads.append((cs, slope, q_aug.T.astype(BF16)))

    def scores(head, start):
        k_aug = jnp.concatenate([k_ref[pl.ds(start, tq), head[0]], pos], axis=1)
        s = jnp.dot(k_aug, head[2], preferred_element_type=F32)
        return s, jnp.max(s, axis=0, keepdims=True)

    def update(head, carry, s, s_max, c, v):
        m, l, acc = carry
        m_new = jnp.maximum(m, s_max + c)
        alpha = jnp.exp(m - m_new)
        p = jnp.exp(s - (m_new - c))
        l = alpha * l + jnp.sum(p, axis=0, keepdims=True)
        pv = lax.dot_general(v, p.astype(BF16), TN_DIMS, preferred_element_type=F32)
        return m_new, l, alpha * acc + pv

    def tile_start(t):
        return pl.multiple_of(t * tq, tq)

    def full_tile(head, carry, s, s_max, start):
        c = head[1] * (start - row0).astype(F32)
        return update(head, carry, s, s_max, c, v_ref[pl.ds(start, tq), head[0]])

    last_full = jnp.maximum(qi - 1, 1)
    init = tuple((jnp.full((1, 2 * tq), NEG, F32), jnp.zeros((1, 2 * tq), F32),
                  jnp.zeros((DA_V_DIM, 2 * tq), F32)) + scores(head, tile_start(jnp.minimum(1, last_full)))
                 for head in heads)

    def body(t, carries):
        out = []
        for head, (m, l, acc, s, s_max) in zip(heads, carries):
            nxt = scores(head, tile_start(t + 1))
            out.append(full_tile(head, (m, l, acc), s, s_max, tile_start(t)) + nxt)
        return tuple(out)

    carries = lax.fori_loop(1, last_full, body, init)

    key_i = lax.broadcasted_iota(jnp.int32, (N_META + tq, 2 * tq), 0) - N_META
    qry_i = lax.broadcasted_iota(jnp.int32, (N_META + tq, 2 * tq), 1) & (tq - 1)
    meta_base = jnp.where(qi >= 1, PAD, -PAD)
    key_row = jnp.where(key_i < 0, meta_base + key_i, row0 + key_i)
    diag_mask = (key_row <= row0 + qry_i) & (key_row >= FIRST_KEY)
    pos_d = jnp.concatenate([pos[:N_META], pos], axis=0)
    lp = lam_ref[...]
    lam = (jnp.exp(jnp.sum(lp[0:1] * lp[1:2], axis=1, keepdims=True))
           - jnp.exp(jnp.sum(lp[2:3] * lp[3:4], axis=1, keepdims=True)) + lambda_init)
    row = row0 + lax.broadcasted_iota(jnp.int32, (tq, 1), 0)
    for head, (m, l, acc, s, s_max) in zip(heads, carries):
        cs, slope, q_t = head
        off = jnp.where(qi >= 2, 0.0, NEG)
        carry = full_tile(head, (m, l, acc), s + off, s_max + off, tile_start(last_full))
        k_d = jnp.concatenate([k_ref[pl.ds(FIRST_KEY, N_META), cs], k_ref[pl.ds(row0, tq), cs]], axis=0)
        v_d = jnp.concatenate([v_ref[pl.ds(FIRST_KEY, N_META), cs], v_ref[pl.ds(row0, tq), cs]], axis=0)
        s_d = jnp.dot(jnp.concatenate([k_d, pos_d], axis=1), q_t, preferred_element_type=F32)
        s_d = s_d + jnp.where(key_i[:, :1] < 0, slope * (FIRST_KEY - row0).astype(F32), 0.0)
        s_d = jnp.where(diag_mask, s_d, NEG)
        m, l, acc = update(head, carry, s_d, jnp.max(s_d, axis=0, keepdims=True), 0.0, v_d)
        o_t = acc / l
        d = (o_t[:, :tq] - lam * o_t[:, tq:]).T
        d = jnp.where(row >= FIRST_KEY, d, 0.0)
        y = _rms_scale(d, sub_ref[...]) * (1.0 - lambda_init)
        o_ref[:, cs] = y.astype(o_ref.dtype)


def diff_attention(qkv, lam_params, subln_w, lambda_init):
    B, Lp, _ = qkv.shape
    hps = ATTN_HEADS_PER_STEP
    w = hps * DA_V_DIM
    n_col = DA_HEADS // hps
    slopes = 2.0 ** (-8.0 * jnp.arange(1, DA_HEADS + 1, dtype=F32) / DA_HEADS)
    kv_spec = lambda sec: pl.BlockSpec((None, Lp, w), lambda b, h, i: (b, 0, sec * n_col + h))
    return pl.pallas_call(
        functools.partial(_attn_body, lambda_init=lambda_init),
        grid=(B, n_col, Lp // ATTN_TQ),
        in_specs=[pl.BlockSpec(memory_space=pltpu.SMEM),
                  pl.BlockSpec((None, ATTN_TQ, w), lambda b, h, i: (b, i, h)),
                  kv_spec(1),
                  kv_spec(2),
                  _resident((4, DA_HEAD_DIM)),
                  _resident((1, DA_V_DIM))],
        out_specs=pl.BlockSpec((None, ATTN_TQ, w), lambda b, h, i: (b, i, h)),
        out_shape=jax.ShapeDtypeStruct((B, Lp, DA_HEADS * DA_V_DIM), BF16),
        compiler_params=pltpu.CompilerParams(
            dimension_semantics=("parallel", "parallel", "arbitrary"), vmem_limit_bytes=VMEM_LIMIT),
        name="diff_attention",
    )(slopes, qkv, qkv, qkv, lam_params.astype(F32), subln_w.reshape(1, DA_V_DIM).astype(F32))


def _gla_tables():
    C = GLA_CHUNK
    i = np.arange(C)[:, None]
    t = np.arange(C)[None, :]
    blocks = [t <= i, t > i]
    level = np.full((C, C), GLA_LEVELS + 1, np.int32)
    level[np.arange(C), np.arange(C)] = GLA_LEVELS
    for l in range(GLA_LEVELS):
        s = C >> (l + 1)
        m = (i // (2 * s)) * 2 * s + s - 1
        upper = (i & s) != 0
        blocks.append(np.where(upper, (t > m) & (t <= i), (t > i) & (t <= m)))
        level[(i > t) & (((i ^ t) >> (GLA_LEVELS - 1 - l)) == 1)] = l
    sums = np.concatenate(blocks, axis=0).astype(np.float32)
    return jnp.asarray(sums, BF16), jnp.asarray(level)


def _gla_body(level_ref, sums_ref, q_ref, k_ref, v_ref, g_ref, gl_ref, nw_ref, o_ref, st_ref):
    C = GLA_CHUNK

    @pl.when(pl.program_id(1) == 0)
    def _():
        st_ref[...] = jnp.zeros_like(st_ref)

    gl = gl_ref[...]
    g_hi = gl.astype(BF16)
    rem = gl - g_hi.astype(F32)
    g_mid = rem.astype(BF16)
    g_lo = (rem - g_mid.astype(F32)).astype(BF16)
    sums = sums_ref[...]
    w_all = (jnp.dot(sums, g_hi, preferred_element_type=F32)
             + jnp.dot(sums, g_mid, preferred_element_type=F32)
             + jnp.dot(sums, g_lo, preferred_element_type=F32))
    level = level_ref[...]

    for h in range(GLA_HEADS):
        ks = slice(h * GLA_HK, (h + 1) * GLA_HK)
        vs = slice(h * GLA_HV, (h + 1) * GLA_HV)
        q = q_ref[:, ks].astype(F32) * (GLA_HK ** -0.5)
        k = k_ref[:, ks].astype(F32)
        v = v_ref[:, vs]
        w = w_all[:, ks]
        cum = w[0:C]
        a = jnp.where(level == GLA_LEVELS, jnp.sum(q * k, axis=1, keepdims=True), 0.0)
        for l in range(GLA_LEVELS):
            e = jnp.exp(w[(2 + l) * C:(3 + l) * C])
            a_l = lax.dot_general((q * e).astype(BF16), (k * e).astype(BF16), NT_DIMS,
                                  preferred_element_type=F32)
            a = jnp.where(level == l, a_l, a)
        state_t = st_ref[h]
        qe = (q * jnp.exp(cum)).astype(BF16)
        o = (jnp.dot(a.astype(BF16), v, preferred_element_type=F32)
             + lax.dot_general(qe, state_t.astype(BF16), NT_DIMS, preferred_element_type=F32))
        kd = (k * jnp.exp(w[C:2 * C])).astype(BF16)
        st_ref[h] = (state_t * jnp.exp(cum[C - 1:C])
                     + lax.dot_general(v, kd, TN_DIMS, preferred_element_type=F32))
        g = g_ref[:, vs].astype(F32)
        y = _rms_scale(o, nw_ref[...]) * (g / (1.0 + jnp.exp(-g)))
        o_ref[:, vs] = y.astype(o_ref.dtype)


def gla_mixer(qkvg, glog, norm_w):
    B, Lp, _ = qkvg.shape
    C = GLA_CHUNK
    sums, level = _gla_tables()
    kw, vw = GLA_KEY_DIM, GLA_VAL_DIM
    return pl.pallas_call(
        _gla_body,
        grid=(B, Lp // C),
        in_specs=[_resident((C, C)),
                  _resident(((2 + GLA_LEVELS) * C, C)),
                  pl.BlockSpec((None, C, kw), lambda b, c: (b, c, 0)),
                  pl.BlockSpec((None, C, kw), lambda b, c: (b, c, 1)),
                  pl.BlockSpec((None, C, vw), lambda b, c: (b, c, 1)),
                  pl.BlockSpec((None, C, vw), lambda b, c: (b, c, 2)),
                  pl.BlockSpec((None, C, kw), lambda b, c: (b, c, 0)),
                  _resident((1, GLA_HV))],
        out_specs=pl.BlockSpec((None, C, vw), lambda b, c: (b, c, 0)),
        out_shape=jax.ShapeDtypeStruct((B, Lp, vw), BF16),
        scratch_shapes=[pltpu.VMEM((GLA_HEADS, GLA_HV, GLA_HK), F32)],
        compiler_params=pltpu.CompilerParams(
            dimension_semantics=("parallel", "arbitrary"), vmem_limit_bytes=VMEM_LIMIT),
        name="gla_mixer",
    )(level, sums, qkvg, qkvg, qkvg, qkvg, glog, norm_w.reshape(1, GLA_HV).astype(F32))


def _out_mlp_body(h_ref, o_ref, wo_ref, nw_ref, wu_ref, wd_ref, out_ref, *, f_chunk):
    h1 = h_ref[...] + jnp.dot(o_ref[...], wo_ref[...], preferred_element_type=F32)
    hn = _rms_scale(h1, nw_ref[...]).astype(BF16)
    out_ref[...] = h1
    for c in range(0, wu_ref.shape[1], f_chunk):
        u = jnp.dot(hn, wu_ref[:, c:c + f_chunk], preferred_element_type=F32)
        u = jnp.square(jnp.maximum(u, 0.0)).astype(BF16)
        out_ref[...] += jnp.dot(u, wd_ref[c:c + f_chunk, :], preferred_element_type=F32)


def out_mlp(h, o, w_out, norm_w, w_up, w_down):
    T, D = h.shape
    F = w_up.shape[1]
    tm = _row_tile(T)
    return pl.pallas_call(
        functools.partial(_out_mlp_body, f_chunk=1024),
        grid=(T // tm,),
        in_specs=[pl.BlockSpec((tm, D), lambda i: (i, 0)),
                  pl.BlockSpec((tm, D), lambda i: (i, 0)),
                  _resident((D, D)),
                  _resident((1, D)),
                  _resident((D, F)),
                  _resident((F, D))],
        out_specs=pl.BlockSpec((tm, D), lambda i: (i, 0)),
        out_shape=jax.ShapeDtypeStruct((T, D), F32),
        compiler_params=pltpu.CompilerParams(
            dimension_semantics=("parallel",), vmem_limit_bytes=VMEM_LIMIT),
        name="out_mlp",
    )(h, o, w_out, norm_w.reshape(1, D), w_up, w_down)


def _final_norm_body(h_ref, nw_ref, o_ref):
    o_ref[...] = _rms_scale(h_ref[...], nw_ref[...])


def final_norm(h, norm_w, seq):
    B, Lp, D = h.shape
    t = PAD
    return pl.pallas_call(
        _final_norm_body,
        grid=(B, seq // t),
        in_specs=[pl.BlockSpec((None, t, D), lambda b, i: (b, i + PAD // t, 0)),
                  _resident((1, D))],
        out_specs=pl.BlockSpec((None, t, D), lambda b, i: (b, i, 0)),
        out_shape=jax.ShapeDtypeStruct((B, seq, D), F32),
        compiler_params=pltpu.CompilerParams(dimension_semantics=("parallel", "parallel")),
        name="final_norm",
    )(h, norm_w.reshape(1, D))


def _lambda_init_for(layer_idx):
    return 0.8 - 0.6 * math.exp(-0.3 * layer_idx)


def kernel(x, meta_tokens, mix_norm_w, attn_w_in, attn_lambda, attn_subln_w, attn_w_out, gla_w_in, gla_w_gate_up, gla_gate_bias, gla_norm_w, gla_w_out, mlp_norm_w, mlp_w_up, mlp_w_down, final_norm_w):
    B, seq, D = x.shape
    Lp = PAD + seq
    T = B * Lp
    lead = jnp.concatenate([jnp.zeros((FIRST_KEY, D), x.dtype), meta_tokens.astype(x.dtype)], axis=0)
    h = jnp.concatenate([jnp.broadcast_to(lead[None], (B, PAD, D)), x], axis=1).reshape(T, D)

    for i in range(DEPTH):
        j = i // 2
        if i % 2 == 0:
            qkv = norm_proj(h, mix_norm_w[i], attn_w_in[j].astype(BF16))
            o = diff_attention(qkv.reshape(B, Lp, 3 * D), attn_lambda[j], attn_subln_w[j],
                               _lambda_init_for(i))
            w_out = attn_w_out[j]
        else:
            n_main = 2 * GLA_KEY_DIM + 2 * GLA_VAL_DIM
            w_in = gla_w_in[j]
            w_gz = jnp.pad(w_in[:, n_main:], ((0, 0), (0, LANE - GLA_GATE_RANK))).astype(BF16)
            w_gu = jnp.pad(gla_w_gate_up[j], ((0, LANE - GLA_GATE_RANK), (0, 0))).astype(BF16)
            qkvg, glog = gla_proj(h, mix_norm_w[i], w_in[:, :n_main].astype(BF16), w_gz, w_gu,
                                  gla_gate_bias[j])
            o = gla_mixer(qkvg.reshape(B, Lp, n_main), glog.reshape(B, Lp, GLA_KEY_DIM), gla_norm_w[j])
            w_out = gla_w_out[j]
        h = out_mlp(h, o.reshape(T, D), w_out.astype(BF16), mlp_norm_w[i],
                    mlp_w_up[i].astype(BF16), mlp_w_down[i].astype(BF16))
    return final_norm(h.reshape(B, Lp, D), final_norm_w, seq)
```

```python
import functools
import math

import numpy as np
import jax
import jax.numpy as jnp
from jax import lax
from jax.experimental import pallas as pl
from jax.experimental.pallas import tpu as pltpu

F32 = jnp.float32
BF16 = jnp.bfloat16

D_MODEL = 1024
DEPTH = 4
N_META = 16
PAD = 256
FIRST_KEY = PAD - N_META
EPS = 1e-6
NEG = -1e30

DA_HEADS = 8
DA_HEAD_DIM = 64
DA_V_DIM = 128
GLA_HEADS = 4
GLA_HK = 128
GLA_HV = 256
GLA_KEY_DIM = 512
GLA_VAL_DIM = 1024
GLA_GATE_RANK = 16
GLA_GATE_NORM = 16.0
GLA_CHUNK = 128
GLA_LEVELS = 7
D_FF = 4096

LANE = 128
VMEM_LIMIT = 50 * 1024 * 1024

NT_DIMS = (((1,), (1,)), ((), ()))
TN_DIMS = (((0,), (0,)), ((), ()))


def _row_tile(n_rows):
    for t in (512, 384, 256, 128):
        if n_rows % t == 0:
            return t
    raise ValueError(f"row count {n_rows} is not a multiple of 128")


def _rms_scale(x, w):
    return x * lax.rsqrt(jnp.mean(x * x, axis=-1, keepdims=True) + EPS) * w


def _resident(shape):
    return pl.BlockSpec(shape, lambda *_: (0,) * len(shape), pipeline_mode=pl.Buffered(1))


def _norm_proj_body(h_ref, nw_ref, w_ref, o_ref, *, n_chunk):
    hn = _rms_scale(h_ref[...], nw_ref[...]).astype(BF16)
    for c in range(0, w_ref.shape[1], n_chunk):
        o_ref[:, c:c + n_chunk] = jnp.dot(
            hn, w_ref[:, c:c + n_chunk], preferred_element_type=F32).astype(o_ref.dtype)


def norm_proj(h, norm_w, w):
    T, D = h.shape
    N = w.shape[1]
    tm = _row_tile(T)
    return pl.pallas_call(
        functools.partial(_norm_proj_body, n_chunk=512),
        grid=(T // tm,),
        in_specs=[pl.BlockSpec((tm, D), lambda i: (i, 0)),
                  _resident((1, D)),
                  _resident((D, N))],
        out_specs=pl.BlockSpec((tm, N), lambda i: (i, 0)),
        out_shape=jax.ShapeDtypeStruct((T, N), BF16),
        compiler_params=pltpu.CompilerParams(
            dimension_semantics=("parallel",), vmem_limit_bytes=VMEM_LIMIT),
        name="norm_proj",
    )(h, norm_w.reshape(1, D), w)


def _gla_proj_body(h_ref, nw_ref, w_ref, wgz_ref, wgu_ref, gb_ref, o_ref, gl_ref, *, n_chunk):
    hn = _rms_scale(h_ref[...], nw_ref[...]).astype(BF16)
    for c in range(0, w_ref.shape[1], n_chunk):
        o_ref[:, c:c + n_chunk] = jnp.dot(
            hn, w_ref[:, c:c + n_chunk], preferred_element_type=F32).astype(o_ref.dtype)
    gz = jnp.dot(hn, wgz_ref[...], preferred_element_type=F32).astype(BF16)
    z = jnp.dot(gz, wgu_ref[...], preferred_element_type=F32) + gb_ref[...]
    gl_ref[...] = (jnp.minimum(z, 0.0) - jnp.log1p(jnp.exp(-jnp.abs(z)))) * (1.0 / GLA_GATE_NORM)


def gla_proj(h, norm_w, w_main, w_gz, w_gate_up, gate_bias):
    T, D = h.shape
    N = w_main.shape[1]
    tm = _row_tile(T)
    return pl.pallas_call(
        functools.partial(_gla_proj_body, n_chunk=512),
        grid=(T // tm,),
        in_specs=[pl.BlockSpec((tm, D), lambda i: (i, 0)),
                  _resident((1, D)),
                  _resident((D, N)),
                  _resident((D, LANE)),
                  _resident((LANE, GLA_KEY_DIM)),
                  _resident((1, GLA_KEY_DIM))],
        out_specs=[pl.BlockSpec((tm, N), lambda i: (i, 0)),
                   pl.BlockSpec((tm, GLA_KEY_DIM), lambda i: (i, 0))],
        out_shape=[jax.ShapeDtypeStruct((T, N), BF16),
                   jax.ShapeDtypeStruct((T, GLA_KEY_DIM), F32)],
        compiler_params=pltpu.CompilerParams(
            dimension_semantics=("parallel",), vmem_limit_bytes=VMEM_LIMIT),
        name="gla_proj",
    )(h, norm_w.reshape(1, D), w_main, w_gz, w_gate_up, gate_bias.reshape(1, GLA_KEY_DIM))


ATTN_TQ = 256
ATTN_HEADS_PER_STEP = 2


def _attn_body(slopes_ref, q_ref, k_ref, v_ref, lam_ref, sub_ref, o_ref, sa_ref, sb_ref, acc_ref, *,
               n_tiles, lambda_init):
    tq = ATTN_TQ
    hp = pl.program_id(1)
    qi = pl.program_id(2)
    row0 = pl.multiple_of(qi * tq, tq)
    lane = lax.broadcasted_iota(jnp.int32, (tq, LANE), 1)
    sub = lax.broadcasted_iota(jnp.int32, (tq, LANE), 0)
    pos = jnp.where(lane == 0, sub, 0).astype(BF16)

    heads = []
    for hh in range(ATTN_HEADS_PER_STEP):
        cs = slice(hh * DA_V_DIM, (hh + 1) * DA_V_DIM)
        slope = slopes_ref[ATTN_HEADS_PER_STEP * hp + hh]
        q = q_ref[:, cs].astype(F32) * (DA_HEAD_DIM ** -0.5)
        feat = jnp.where(lane == 0, slope, 0.0)
        q_aug = jnp.concatenate(
            [jnp.concatenate([jnp.where(lane < DA_HEAD_DIM, q, 0.0), feat], axis=1),
             jnp.concatenate([jnp.where(lane >= DA_HEAD_DIM, q, 0.0), feat], axis=1)], axis=0)
        heads.append((cs, slope, q_aug.T.astype(BF16)))

    def scores(head, start):
        k_aug = jnp.concatenate([k_ref[pl.ds(start, tq), head[0]], pos], axis=1)
        s = jnp.dot(k_aug, head[2], preferred_element_type=F32)
        return s, jnp.max(s, axis=0, keepdims=True)

    def update(hh, carry, s, s_max, c, v):
        m, l = carry
        m_new = jnp.maximum(m, s_max + c)
        alpha = jnp.exp(m - m_new)
        p = jnp.exp(s - (m_new - c))
        l = alpha * l + jnp.sum(p, axis=0, keepdims=True)
        pv = lax.dot_general(v, p.astype(BF16), TN_DIMS, preferred_element_type=F32)
        acc_ref[hh] = alpha * acc_ref[hh] + pv
        return m_new, l

    def tile_start(t):
        return pl.multiple_of(t * tq, tq)

    def full_tile(hh, carry, s, s_max, start):
        c = heads[hh][1] * (start - row0).astype(F32)
        return update(hh, carry, s, s_max, c, v_ref[pl.ds(start, tq), heads[hh][0]])

    n_full = qi - 1
    trips = jnp.maximum(n_full - 1, 0) // 2
    carries = []
    for hh, head in enumerate(heads):
        acc_ref[hh] = jnp.zeros((DA_V_DIM, 2 * tq), F32)
        s, s_max = scores(head, tile_start(1))
        sa_ref[hh] = s
        carries.append((jnp.full((1, 2 * tq), NEG, F32), jnp.zeros((1, 2 * tq), F32), s_max))

    def issue(buf_ref, t):
        maxes = []
        for hh, head in enumerate(heads):
            s, s_max = scores(head, tile_start(t))
            buf_ref[hh] = s
            maxes.append(s_max)
        return maxes

    def consume(buf_ref, t, stats, maxes, off=None):
        out = []
        for hh, ((m, l), s_max) in enumerate(zip(stats, maxes)):
            s = buf_ref[hh]
            if off is not None:
                s, s_max = s + off, s_max + off
            out.append(full_tile(hh, (m, l), s, s_max, tile_start(t)))
        return out

    def body(i, carries):
        t = 1 + 2 * i
        stats = [(m, l) for m, l, _ in carries]
        max_b = issue(sb_ref, t + 1)
        stats = consume(sa_ref, t, stats, [c[2] for c in carries])
        max_a = issue(sa_ref, t + 2)
        stats = consume(sb_ref, t + 1, stats, max_b)
        return tuple((m, l, mx) for (m, l), mx in zip(stats, max_a))

    carries = lax.fori_loop(0, trips, body, tuple(carries))

    t_a = 1 + 2 * trips
    t_b = jnp.minimum(t_a + 1, n_tiles - 1)
    remaining = n_full - 2 * trips
    stats = [(m, l) for m, l, _ in carries]
    max_b = issue(sb_ref, t_b)
    stats = consume(sa_ref, t_a, stats, [c[2] for c in carries], jnp.where(remaining >= 1, 0.0, NEG))
    stats = consume(sb_ref, t_b, stats, max_b, jnp.where(remaining >= 2, 0.0, NEG))

    key_i = lax.broadcasted_iota(jnp.int32, (N_META + tq, 2 * tq), 0) - N_META
    qry_i = lax.broadcasted_iota(jnp.int32, (N_META + tq, 2 * tq), 1) & (tq - 1)
    meta_base = jnp.where(qi >= 1, PAD, -PAD)
    key_row = jnp.where(key_i < 0, meta_base + key_i, row0 + key_i)
    diag_mask = (key_row <= row0 + qry_i) & (key_row >= FIRST_KEY)
    pos_d = jnp.concatenate([pos[:N_META], pos], axis=0)
    lp = lam_ref[...]
    lam = (jnp.exp(jnp.sum(lp[0:1] * lp[1:2], axis=1, keepdims=True))
           - jnp.exp(jnp.sum(lp[2:3] * lp[3:4], axis=1, keepdims=True)) + lambda_init)
    row = row0 + lax.broadcasted_iota(jnp.int32, (tq, 1), 0)
    for hh, (head, carry) in enumerate(zip(heads, stats)):
        cs, slope, q_t = head
        k_d = jnp.concatenate([k_ref[pl.ds(FIRST_KEY, N_META), cs], k_ref[pl.ds(row0, tq), cs]], axis=0)
        v_d = jnp.concatenate([v_ref[pl.ds(FIRST_KEY, N_META), cs], v_ref[pl.ds(row0, tq), cs]], axis=0)
        s_d = jnp.dot(jnp.concatenate([k_d, pos_d], axis=1), q_t, preferred_element_type=F32)
        s_d = s_d + jnp.where(key_i[:, :1] < 0, slope * (FIRST_KEY - row0).astype(F32), 0.0)
        s_d = jnp.where(diag_mask, s_d, NEG)
        m, l = update(hh, carry, s_d, jnp.max(s_d, axis=0, keepdims=True), 0.0, v_d)
        o_t = acc_ref[hh] / l
        d = (o_t[:, :tq] - lam * o_t[:, tq:]).T
        d = jnp.where(row >= FIRST_KEY, d, 0.0)
        y = _rms_scale(d, sub_ref[...]) * (1.0 - lambda_init)
        o_ref[:, cs] = y.astype(o_ref.dtype)


def diff_attention(qkv, lam_params, subln_w, lambda_init):
    B, Lp, _ = qkv.shape
    hps = ATTN_HEADS_PER_STEP
    w = hps * DA_V_DIM
    n_col = DA_HEADS // hps
    slopes = 2.0 ** (-8.0 * jnp.arange(1, DA_HEADS + 1, dtype=F32) / DA_HEADS)
    kv_spec = lambda sec: pl.BlockSpec((None, Lp, w), lambda b, h, i: (b, 0, sec * n_col + h))
    return pl.pallas_call(
        functools.partial(_attn_body, n_tiles=Lp // ATTN_TQ, lambda_init=lambda_init),
        grid=(B, n_col, Lp // ATTN_TQ),
        in_specs=[pl.BlockSpec(memory_space=pltpu.SMEM),
                  pl.BlockSpec((None, ATTN_TQ, w), lambda b, h, i: (b, i, h)),
                  kv_spec(1),
                  kv_spec(2),
                  _resident((4, DA_HEAD_DIM)),
                  _resident((1, DA_V_DIM))],
        out_specs=pl.BlockSpec((None, ATTN_TQ, w), lambda b, h, i: (b, i, h)),
        out_shape=jax.ShapeDtypeStruct((B, Lp, DA_HEADS * DA_V_DIM), BF16),
        scratch_shapes=[pltpu.VMEM((hps, ATTN_TQ, 2 * ATTN_TQ), F32),
                        pltpu.VMEM((hps, ATTN_TQ, 2 * ATTN_TQ), F32),
                        pltpu.VMEM((hps, DA_V_DIM, 2 * ATTN_TQ), F32)],
        compiler_params=pltpu.CompilerParams(
            dimension_semantics=("parallel", "parallel", "arbitrary"), vmem_limit_bytes=VMEM_LIMIT),
        name="diff_attention",
    )(slopes, qkv, qkv, qkv, lam_params.astype(F32), subln_w.reshape(1, DA_V_DIM).astype(F32))


def _gla_tables():
    C = GLA_CHUNK
    i = np.arange(C)[:, None]
    t = np.arange(C)[None, :]
    blocks = [t <= i, t > i]
    level = np.full((C, C), GLA_LEVELS + 1, np.int32)
    level[np.arange(C), np.arange(C)] = GLA_LEVELS
    for l in range(GLA_LEVELS):
        s = C >> (l + 1)
        m = (i // (2 * s)) * 2 * s + s - 1
        upper = (i & s) != 0
        blocks.append(np.where(upper, (t > m) & (t <= i), (t > i) & (t <= m)))
        level[(i > t) & (((i ^ t) >> (GLA_LEVELS - 1 - l)) == 1)] = l
    sums = np.concatenate(blocks, axis=0).astype(np.float32)
    return jnp.asarray(sums, BF16), jnp.asarray(level)


def _gla_body(level_ref, sums_ref, q_ref, k_ref, v_ref, g_ref, gl_ref, nw_ref, o_ref, st_ref):
    C = GLA_CHUNK

    @pl.when(pl.program_id(1) == 0)
    def _():
        st_ref[...] = jnp.zeros_like(st_ref)

    gl = gl_ref[...]
    g_hi = gl.astype(BF16)
    rem = gl - g_hi.astype(F32)
    g_mid = rem.astype(BF16)
    g_lo = (rem - g_mid.astype(F32)).astype(BF16)
    sums = sums_ref[...]
    w_all = (jnp.dot(sums, g_hi, preferred_element_type=F32)
             + jnp.dot(sums, g_mid, preferred_element_type=F32)
             + jnp.dot(sums, g_lo, preferred_element_type=F32))
    level = level_ref[...]

    for h in range(GLA_HEADS):
        ks = slice(h * GLA_HK, (h + 1) * GLA_HK)
        vs = slice(h * GLA_HV, (h + 1) * GLA_HV)
        q = q_ref[:, ks].astype(F32) * (GLA_HK ** -0.5)
        k = k_ref[:, ks].astype(F32)
        v = v_ref[:, vs]
        w = w_all[:, ks]
        cum = w[0:C]
        a = jnp.where(level == GLA_LEVELS, jnp.sum(q * k, axis=1, keepdims=True), 0.0)
        for l in range(GLA_LEVELS):
            e = jnp.exp(w[(2 + l) * C:(3 + l) * C])
            a_l = lax.dot_general((q * e).astype(BF16), (k * e).astype(BF16), NT_DIMS,
                                  preferred_element_type=F32)
            a = jnp.where(level == l, a_l, a)
        state_t = st_ref[h]
        qe = (q * jnp.exp(cum)).astype(BF16)
        o = (jnp.dot(a.astype(BF16), v, preferred_element_type=F32)
             + lax.dot_general(qe, state_t.astype(BF16), NT_DIMS, preferred_element_type=F32))
        kd = (k * jnp.exp(w[C:2 * C])).astype(BF16)
        st_ref[h] = (state_t * jnp.exp(cum[C - 1:C])
                     + lax.dot_general(v, kd, TN_DIMS, preferred_element_type=F32))
        g = g_ref[:, vs].astype(F32)
        y = _rms_scale(o, nw_ref[...]) * (g / (1.0 + jnp.exp(-g)))
        o_ref[:, vs] = y.astype(o_ref.dtype)


def gla_mixer(qkvg, glog, norm_w):
    B, Lp, _ = qkvg.shape
    C = GLA_CHUNK
    sums, level = _gla_tables()
    kw, vw = GLA_KEY_DIM, GLA_VAL_DIM
    return pl.pallas_call(
        _gla_body,
        grid=(B, Lp // C),
        in_specs=[_resident((C, C)),
                  _resident(((2 + GLA_LEVELS) * C, C)),
                  pl.BlockSpec((None, C, kw), lambda b, c: (b, c, 0)),
                  pl.BlockSpec((None, C, kw), lambda b, c: (b, c, 1)),
                  pl.BlockSpec((None, C, vw), lambda b, c: (b, c, 1)),
                  pl.BlockSpec((None, C, vw), lambda b, c: (b, c, 2)),
                  pl.BlockSpec((None, C, kw), lambda b, c: (b, c, 0)),
                  _resident((1, GLA_HV))],
        out_specs=pl.BlockSpec((None, C, vw), lambda b, c: (b, c, 0)),
        out_shape=jax.ShapeDtypeStruct((B, Lp, vw), BF16),
        scratch_shapes=[pltpu.VMEM((GLA_HEADS, GLA_HV, GLA_HK), F32)],
        compiler_params=pltpu.CompilerParams(
            dimension_semantics=("parallel", "arbitrary"), vmem_limit_bytes=VMEM_LIMIT),
        name="gla_mixer",
    )(level, sums, qkvg, qkvg, qkvg, qkvg, glog, norm_w.reshape(1, GLA_HV).astype(F32))


def _out_mlp_body(h_ref, o_ref, wo_ref, nw_ref, wu_ref, wd_ref, out_ref, *, f_chunk):
    h1 = h_ref[...] + jnp.dot(o_ref[...], wo_ref[...], preferred_element_type=F32)
    hn = _rms_scale(h1, nw_ref[...]).astype(BF16)
    out_ref[...] = h1
    for c in range(0, wu_ref.shape[1], f_chunk):
        u = jnp.dot(hn, wu_ref[:, c:c + f_chunk], preferred_element_type=F32)
        u = jnp.square(jnp.maximum(u, 0.0)).astype(BF16)
        out_ref[...] += jnp.dot(u, wd_ref[c:c + f_chunk, :], preferred_element_type=F32)


def out_mlp(h, o, w_out, norm_w, w_up, w_down):
    T, D = h.shape
    F = w_up.shape[1]
    tm = _row_tile(T)
    return pl.pallas_call(
        functools.partial(_out_mlp_body, f_chunk=1024),
        grid=(T // tm,),
        in_specs=[pl.BlockSpec((tm, D), lambda i: (i, 0)),
                  pl.BlockSpec((tm, D), lambda i: (i, 0)),
                  _resident((D, D)),
                  _resident((1, D)),
                  _resident((D, F)),
                  _resident((F, D))],
        out_specs=pl.BlockSpec((tm, D), lambda i: (i, 0)),
        out_shape=jax.ShapeDtypeStruct((T, D), F32),
        compiler_params=pltpu.CompilerParams(
            dimension_semantics=("parallel",), vmem_limit_bytes=VMEM_LIMIT),
        name="out_mlp",
    )(h, o, w_out, norm_w.reshape(1, D), w_up, w_down)


def _final_norm_body(h_ref, nw_ref, o_ref):
    o_ref[...] = _rms_scale(h_ref[...], nw_ref[...])


def final_norm(h, norm_w, seq):
    B, Lp, D = h.shape
    t = PAD
    return pl.pallas_call(
        _final_norm_body,
        grid=(B, seq // t),
        in_specs=[pl.BlockSpec((None, t, D), lambda b, i: (b, i + PAD // t, 0)),
                  _resident((1, D))],
        out_specs=pl.BlockSpec((None, t, D), lambda b, i: (b, i, 0)),
        out_shape=jax.ShapeDtypeStruct((B, seq, D), F32),
        compiler_params=pltpu.CompilerParams(dimension_semantics=("parallel", "parallel")),
        name="final_norm",
    )(h, norm_w.reshape(1, D))


def _lambda_init_for(layer_idx):
    return 0.8 - 0.6 * math.exp(-0.3 * layer_idx)


def kernel(x, meta_tokens, mix_norm_w, attn_w_in, attn_lambda, attn_subln_w, attn_w_out, gla_w_in, gla_w_gate_up, gla_gate_bias, gla_norm_w, gla_w_out, mlp_norm_w, mlp_w_up, mlp_w_down, final_norm_w):
    B, seq, D = x.shape
    Lp = PAD + seq
    T = B * Lp
    lead = jnp.concatenate([jnp.zeros((FIRST_KEY, D), x.dtype), meta_tokens.astype(x.dtype)], axis=0)
    h = jnp.concatenate([jnp.broadcast_to(lead[None], (B, PAD, D)), x], axis=1).reshape(T, D)

    for i in range(DEPTH):
        j = i // 2
        if i % 2 == 0:
            qkv = norm_proj(h, mix_norm_w[i], attn_w_in[j].astype(BF16))
            o = diff_attention(qkv.reshape(B, Lp, 3 * D), attn_lambda[j], attn_subln_w[j],
                               _lambda_init_for(i))
            w_out = attn_w_out[j]
        else:
            n_main = 2 * GLA_KEY_DIM + 2 * GLA_VAL_DIM
            w_in = gla_w_in[j]
            w_gz = jnp.pad(w_in[:, n_main:], ((0, 0), (0, LANE - GLA_GATE_RANK))).astype(BF16)
            w_gu = jnp.pad(gla_w_gate_up[j], ((0, LANE - GLA_GATE_RANK), (0, 0))).astype(BF16)
            qkvg, glog = gla_proj(h, mix_norm_w[i], w_in[:, :n_main].astype(BF16), w_gz, w_gu,
                                  gla_gate_bias[j])
            o = gla_mixer(qkvg.reshape(B, Lp, n_main), glog.reshape(B, Lp, GLA_KEY_DIM), gla_norm_w[j])
            w_out = gla_w_out[j]
        h = out_mlp(h, o.reshape(T, D), w_out.astype(BF16), mlp_norm_w[i],
                    mlp_w_up[i].astype(BF16), mlp_w_down[i].astype(BF16))
    return final_norm(h.reshape(B, Lp, D), final_norm_w, seq)
```

```python
import functools
import math

import numpy as np
import jax
import jax.numpy as jnp
from jax import lax
from jax.experimental import pallas as pl
from jax.experimental.pallas import tpu as pltpu

F32 = jnp.float32
BF16 = jnp.bfloat16

D_MODEL = 1024
DEPTH = 4
N_META = 16
PAD = 256
FIRST_KEY = PAD - N_META
EPS = 1e-6
NEG = -1e30

DA_HEADS = 8
DA_HEAD_DIM = 64
DA_V_DIM = 128
GLA_HEADS = 4
GLA_HK = 128
GLA_HV = 256
GLA_KEY_DIM = 512
GLA_VAL_DIM = 1024
GLA_GATE_RANK = 16
GLA_GATE_NORM = 16.0
GLA_CHUNK = 128
GLA_LEVELS = 7
D_FF = 4096

LANE = 128
VMEM_LIMIT = 50 * 1024 * 1024

NT_DIMS = (((1,), (1,)), ((), ()))
TN_DIMS = (((0,), (0,)), ((), ()))


def _row_tile(n_rows):
    for t in (512, 384, 256, 128):
        if n_rows % t == 0:
            return t
    raise ValueError(f"row count {n_rows} is not a multiple of 128")


def _rms_scale(x, w):
    return x * lax.rsqrt(jnp.mean(x * x, axis=-1, keepdims=True) + EPS) * w


def _resident(shape):
    return pl.BlockSpec(shape, lambda *_: (0,) * len(shape), pipeline_mode=pl.Buffered(1))


def _norm_proj_body(h_ref, nw_ref, w_ref, o_ref, *, n_chunk):
    hn = _rms_scale(h_ref[...], nw_ref[...]).astype(BF16)
    for c in range(0, w_ref.shape[1], n_chunk):
        o_ref[:, c:c + n_chunk] = jnp.dot(
            hn, w_ref[:, c:c + n_chunk], preferred_element_type=F32).astype(o_ref.dtype)


def norm_proj(h, norm_w, w):
    T, D = h.shape
    N = w.shape[1]
    tm = _row_tile(T)
    return pl.pallas_call(
        functools.partial(_norm_proj_body, n_chunk=512),
        grid=(T // tm,),
        in_specs=[pl.BlockSpec((tm, D), lambda i: (i, 0)),
                  _resident((1, D)),
                  _resident((D, N))],
        out_specs=pl.BlockSpec((tm, N), lambda i: (i, 0)),
        out_shape=jax.ShapeDtypeStruct((T, N), BF16),
        compiler_params=pltpu.CompilerParams(
            dimension_semantics=("parallel",), vmem_limit_bytes=VMEM_LIMIT),
        name="norm_proj",
    )(h, norm_w.reshape(1, D), w)


def _gla_proj_body(h_ref, nw_ref, w_ref, wgz_ref, wgu_ref, gb_ref, o_ref, gl_ref, *, n_chunk):
    hn = _rms_scale(h_ref[...], nw_ref[...]).astype(BF16)
    for c in range(0, w_ref.shape[1], n_chunk):
        o_ref[:, c:c + n_chunk] = jnp.dot(
            hn, w_ref[:, c:c + n_chunk], preferred_element_type=F32).astype(o_ref.dtype)
    gz = jnp.dot(hn, wgz_ref[...], preferred_element_type=F32).astype(BF16)
    z = jnp.dot(gz, wgu_ref[...], preferred_element_type=F32) + gb_ref[...]
    gl_ref[...] = (jnp.minimum(z, 0.0) - jnp.log1p(jnp.exp(-jnp.abs(z)))) * (1.0 / GLA_GATE_NORM)


def gla_proj(h, norm_w, w_main, w_gz, w_gate_up, gate_bias):
    T, D = h.shape
    N = w_main.shape[1]
    tm = _row_tile(T)
    return pl.pallas_call(
        functools.partial(_gla_proj_body, n_chunk=512),
        grid=(T // tm,),
        in_specs=[pl.BlockSpec((tm, D), lambda i: (i, 0)),
                  _resident((1, D)),
                  _resident((D, N)),
                  _resident((D, LANE)),
                  _resident((LANE, GLA_KEY_DIM)),
                  _resident((1, GLA_KEY_DIM))],
        out_specs=[pl.BlockSpec((tm, N), lambda i: (i, 0)),
                   pl.BlockSpec((tm, GLA_KEY_DIM), lambda i: (i, 0))],
        out_shape=[jax.ShapeDtypeStruct((T, N), BF16),
                   jax.ShapeDtypeStruct((T, GLA_KEY_DIM), F32)],
        compiler_params=pltpu.CompilerParams(
            dimension_semantics=("parallel",), vmem_limit_bytes=VMEM_LIMIT),
        name="gla_proj",
    )(h, norm_w.reshape(1, D), w_main, w_gz, w_gate_up, gate_bias.reshape(1, GLA_KEY_DIM))


ATTN_TQ = 256
ATTN_HEADS_PER_STEP = 4
ATTN_ACC_ROWS = DA_V_DIM + 16
SLOPE_PIECES = 3
LOG2E = math.log2(math.e)
ATTN_Q_SCALE = DA_HEAD_DIM ** -0.5 * LOG2E


def _attn_body(slopes_ref, q_ref, k_ref, v_ref, lam_ref, sub_ref, o_ref, sa_ref, sb_ref, acc_ref, *,
               n_tiles, lambda_init):
    tq = ATTN_TQ
    hp = pl.program_id(1)
    qi = pl.program_id(2)
    row0 = pl.multiple_of(qi * tq, tq)
    lane = lax.broadcasted_iota(jnp.int32, (tq, LANE), 1)
    sub = lax.broadcasted_iota(jnp.int32, (tq, LANE), 0)
    pos = jnp.where(lane < SLOPE_PIECES, sub, 0).astype(BF16)
    ones_rows = jnp.ones((ATTN_ACC_ROWS - DA_V_DIM, tq), BF16)

    heads = []
    for hh in range(ATTN_HEADS_PER_STEP):
        cs = slice(hh * DA_V_DIM, (hh + 1) * DA_V_DIM)
        h = ATTN_HEADS_PER_STEP * hp + hh
        q = q_ref[:, cs].astype(F32)
        feat = jnp.zeros((tq, LANE), F32)
        for j in range(SLOPE_PIECES):
            feat = jnp.where(lane == j, slopes_ref[h, j], feat)
        q_aug = jnp.concatenate(
            [jnp.concatenate([jnp.where(lane < DA_HEAD_DIM, q, 0.0), feat], axis=1),
             jnp.concatenate([jnp.where(lane >= DA_HEAD_DIM, q, 0.0), feat], axis=1)], axis=0)
        heads.append((cs, slopes_ref[h, SLOPE_PIECES], q_aug.T.astype(BF16)))

    def scores(head, start):
        k_aug = jnp.concatenate([k_ref[pl.ds(start, tq), head[0]], pos], axis=1)
        s = jnp.dot(k_aug, head[2], preferred_element_type=F32)
        return s, jnp.max(s, axis=0, keepdims=True)

    def update(hh, m, s, s_max, c, pv_and_sum):
        m_new = jnp.maximum(m, s_max + c)
        alpha = jnp.exp2(m - m_new)
        p = jnp.exp2(s - (m_new - c))
        acc_ref[hh] = alpha * acc_ref[hh] + pv_and_sum(p)
        return m_new

    def tile_start(t):
        return pl.multiple_of(t * tq, tq)

    def full_tile(hh, m, s, s_max, start):
        c = heads[hh][1] * (start - row0).astype(F32)
        v_t = jnp.concatenate([v_ref[pl.ds(start, tq), heads[hh][0]].T, ones_rows], axis=0)
        return update(hh, m, s, s_max, c,
                      lambda p: jnp.dot(v_t, p.astype(BF16), preferred_element_type=F32))

    n_full = qi - 1
    trips = jnp.maximum(n_full - 1, 0) // 2
    carries = []
    for hh, head in enumerate(heads):
        acc_ref[hh] = jnp.zeros((ATTN_ACC_ROWS, 2 * tq), F32)
        s, s_max = scores(head, tile_start(1))
        sa_ref[hh] = s
        carries.append((jnp.full((1, 2 * tq), NEG, F32), s_max))

    def issue(buf_ref, t):
        maxes = []
        for hh, head in enumerate(heads):
            s, s_max = scores(head, tile_start(t))
            buf_ref[hh] = s
            maxes.append(s_max)
        return maxes

    def consume(buf_ref, t, ms, maxes, off=None):
        out = []
        for hh, (m, s_max) in enumerate(zip(ms, maxes)):
            s = buf_ref[hh]
            if off is not None:
                s, s_max = s + off, s_max + off
            out.append(full_tile(hh, m, s, s_max, tile_start(t)))
        return out

    def body(i, carries):
        t = 1 + 2 * i
        max_b = issue(sb_ref, t + 1)
        ms = consume(sa_ref, t, [c[0] for c in carries], [c[1] for c in carries])
        max_a = issue(sa_ref, t + 2)
        ms = consume(sb_ref, t + 1, ms, max_b)
        return tuple(zip(ms, max_a))

    carries = lax.fori_loop(0, trips, body, tuple(carries))

    t_a = 1 + 2 * trips
    t_b = jnp.minimum(t_a + 1, n_tiles - 1)
    remaining = n_full - 2 * trips
    max_b = issue(sb_ref, t_b)
    ms = consume(sa_ref, t_a, [c[0] for c in carries], [c[1] for c in carries],
                 jnp.where(remaining >= 1, 0.0, NEG))
    ms = consume(sb_ref, t_b, ms, max_b, jnp.where(remaining >= 2, 0.0, NEG))

    key_i = lax.broadcasted_iota(jnp.int32, (N_META + tq, 2 * tq), 0) - N_META
    qry_i = lax.broadcasted_iota(jnp.int32, (N_META + tq, 2 * tq), 1) & (tq - 1)
    meta_base = jnp.where(qi >= 1, PAD, -PAD)
    key_row = jnp.where(key_i < 0, meta_base + key_i, row0 + key_i)
    diag_mask = (key_row <= row0 + qry_i) & (key_row >= FIRST_KEY)
    pos_d = jnp.concatenate([pos[:N_META], pos], axis=0)
    lp = lam_ref[...]
    lam = (jnp.exp(jnp.sum(lp[0:1] * lp[1:2], axis=1, keepdims=True))
           - jnp.exp(jnp.sum(lp[2:3] * lp[3:4], axis=1, keepdims=True)) + lambda_init)
    row = row0 + lax.broadcasted_iota(jnp.int32, (tq, 1), 0)
    sum_rows = (ATTN_ACC_ROWS - DA_V_DIM, 2 * tq)
    for hh, (head, m) in enumerate(zip(heads, ms)):
        cs, slope, q_t = head
        k_d = jnp.concatenate([k_ref[pl.ds(FIRST_KEY, N_META), cs], k_ref[pl.ds(row0, tq), cs]], axis=0)
        v_d = jnp.concatenate([v_ref[pl.ds(FIRST_KEY, N_META), cs], v_ref[pl.ds(row0, tq), cs]], axis=0)
        s_d = jnp.dot(jnp.concatenate([k_d, pos_d], axis=1), q_t, preferred_element_type=F32)
        s_d = s_d + jnp.where(key_i[:, :1] < 0, slope * (FIRST_KEY - row0).astype(F32), 0.0)
        s_d = jnp.where(diag_mask, s_d, NEG)
        update(hh, m, s_d, jnp.max(s_d, axis=0, keepdims=True), 0.0,
               lambda p: jnp.concatenate(
                   [lax.dot_general(v_d, p.astype(BF16), TN_DIMS, preferred_element_type=F32),
                    jnp.broadcast_to(jnp.sum(p, axis=0, keepdims=True), sum_rows)], axis=0))
        acc = acc_ref[hh]
        o_t = acc[:DA_V_DIM] / acc[DA_V_DIM:DA_V_DIM + 1]
        d = (o_t[:, :tq] - lam * o_t[:, tq:]).T
        d = jnp.where(row >= FIRST_KEY, d, 0.0)
        y = _rms_scale(d, sub_ref[...]) * (1.0 - lambda_init)
        o_ref[:, cs] = y.astype(o_ref.dtype)


def diff_attention(qkv, lam_params, subln_w, lambda_init):
    B, Lp, _ = qkv.shape
    hps = ATTN_HEADS_PER_STEP
    w = hps * DA_V_DIM
    n_col = DA_HEADS // hps
    slope = 2.0 ** (-8.0 * jnp.arange(1, DA_HEADS + 1, dtype=F32) / DA_HEADS) * LOG2E
    pieces, rest = [], slope
    for _ in range(SLOPE_PIECES):
        pieces.append(rest.astype(BF16).astype(F32))
        rest = rest - pieces[-1]
    slopes = jnp.stack(pieces + [slope], axis=1)
    kv_spec = lambda sec: pl.BlockSpec((None, Lp, w), lambda b, h, i: (b, 0, sec * n_col + h))
    return pl.pallas_call(
        functools.partial(_attn_body, n_tiles=Lp // ATTN_TQ, lambda_init=lambda_init),
        grid=(B, n_col, Lp // ATTN_TQ),
        in_specs=[pl.BlockSpec(memory_space=pltpu.SMEM),
                  pl.BlockSpec((None, ATTN_TQ, w), lambda b, h, i: (b, i, h)),
                  kv_spec(1),
                  kv_spec(2),
                  _resident((4, DA_HEAD_DIM)),
                  _resident((1, DA_V_DIM))],
        out_specs=pl.BlockSpec((None, ATTN_TQ, w), lambda b, h, i: (b, i, h)),
        out_shape=jax.ShapeDtypeStruct((B, Lp, DA_HEADS * DA_V_DIM), BF16),
        scratch_shapes=[pltpu.VMEM((hps, ATTN_TQ, 2 * ATTN_TQ), F32),
                        pltpu.VMEM((hps, ATTN_TQ, 2 * ATTN_TQ), F32),
                        pltpu.VMEM((hps, ATTN_ACC_ROWS, 2 * ATTN_TQ), F32)],
        compiler_params=pltpu.CompilerParams(
            dimension_semantics=("parallel", "parallel", "arbitrary"), vmem_limit_bytes=VMEM_LIMIT),
        name="diff_attention",
    )(slopes, qkv, qkv, qkv, lam_params.astype(F32), subln_w.reshape(1, DA_V_DIM).astype(F32))


def _gla_tables():
    C = GLA_CHUNK
    i = np.arange(C)[:, None]
    t = np.arange(C)[None, :]
    blocks = [t <= i, t > i]
    level = np.full((C, C), GLA_LEVELS + 1, np.int32)
    level[np.arange(C), np.arange(C)] = GLA_LEVELS
    for l in range(GLA_LEVELS):
        s = C >> (l + 1)
        m = (i // (2 * s)) * 2 * s + s - 1
        upper = (i & s) != 0
        blocks.append(np.where(upper, (t > m) & (t <= i), (t > i) & (t <= m)))
        level[(i > t) & (((i ^ t) >> (GLA_LEVELS - 1 - l)) == 1)] = l
    sums = np.concatenate(blocks, axis=0).astype(np.float32)
    sums = np.concatenate([sums, sums], axis=1)
    return jnp.asarray(sums, BF16), jnp.asarray(level)


def _gla_body(level_ref, sums_ref, q_ref, k_ref, v_ref, g_ref, gl_ref, nw_ref, o_ref, st_ref):
    C = GLA_CHUNK

    @pl.when(pl.program_id(1) == 0)
    def _():
        st_ref[...] = jnp.zeros_like(st_ref)

    gl = gl_ref[...]
    g_hi = gl.astype(BF16)
    g_lo = (gl - g_hi.astype(F32)).astype(BF16)
    w_all = jnp.dot(sums_ref[...], jnp.concatenate([g_hi, g_lo], axis=0),
                    preferred_element_type=F32)
    level = level_ref[...]

    for h in range(GLA_HEADS):
        ks = slice(h * GLA_HK, (h + 1) * GLA_HK)
        vs = slice(h * GLA_HV, (h + 1) * GLA_HV)
        q = q_ref[:, ks].astype(F32) * (GLA_HK ** -0.5)
        k = k_ref[:, ks].astype(F32)
        v = v_ref[:, vs]
        w = w_all[:, ks]
        cum = w[0:C]
        a = jnp.where(level == GLA_LEVELS, jnp.sum(q * k, axis=1, keepdims=True), 0.0)
        for l in range(GLA_LEVELS):
            e = jnp.exp(w[(2 + l) * C:(3 + l) * C])
            a_l = lax.dot_general((q * e).astype(BF16), (k * e).astype(BF16), NT_DIMS,
                                  preferred_element_type=F32)
            a = jnp.where(level == l, a_l, a)
        state_t = st_ref[h]
        qe = (q * jnp.exp(cum)).astype(BF16)
        o = (jnp.dot(a.astype(BF16), v, preferred_element_type=F32)
             + lax.dot_general(qe, state_t.astype(BF16), NT_DIMS, preferred_element_type=F32))
        kd = (k * jnp.exp(w[C:2 * C])).astype(BF16)
        st_ref[h] = (state_t * jnp.exp(cum[C - 1:C])
                     + lax.dot_general(v, kd, TN_DIMS, preferred_element_type=F32))
        g = g_ref[:, vs].astype(F32)
        y = _rms_scale(o, nw_ref[...]) * (g / (1.0 + jnp.exp(-g)))
        o_ref[:, vs] = y.astype(o_ref.dtype)


def gla_mixer(qkvg, glog, norm_w):
    B, Lp, _ = qkvg.shape
    C = GLA_CHUNK
    sums, level = _gla_tables()
    kw, vw = GLA_KEY_DIM, GLA_VAL_DIM
    return pl.pallas_call(
        _gla_body,
        grid=(B, Lp // C),
        in_specs=[_resident((C, C)),
                  _resident(((2 + GLA_LEVELS) * C, 2 * C)),
                  pl.BlockSpec((None, C, kw), lambda b, c: (b, c, 0)),
                  pl.BlockSpec((None, C, kw), lambda b, c: (b, c, 1)),
                  pl.BlockSpec((None, C, vw), lambda b, c: (b, c, 1)),
                  pl.BlockSpec((None, C, vw), lambda b, c: (b, c, 2)),
                  pl.BlockSpec((None, C, kw), lambda b, c: (b, c, 0)),
                  _resident((1, GLA_HV))],
        out_specs=pl.BlockSpec((None, C, vw), lambda b, c: (b, c, 0)),
        out_shape=jax.ShapeDtypeStruct((B, Lp, vw), BF16),
        scratch_shapes=[pltpu.VMEM((GLA_HEADS, GLA_HV, GLA_HK), F32)],
        compiler_params=pltpu.CompilerParams(
            dimension_semantics=("parallel", "arbitrary"), vmem_limit_bytes=VMEM_LIMIT),
        name="gla_mixer",
    )(level, sums, qkvg, qkvg, qkvg, qkvg, glog, norm_w.reshape(1, GLA_HV).astype(F32))


def _out_mlp_body(h_ref, o_ref, wo_ref, nw_ref, wu_ref, wd_ref, out_ref, *, f_chunk):
    h1 = h_ref[...] + jnp.dot(o_ref[...], wo_ref[...], preferred_element_type=F32)
    hn = _rms_scale(h1, nw_ref[...]).astype(BF16)
    out_ref[...] = h1
    for c in range(0, wu_ref.shape[1], f_chunk):
        u = jnp.dot(hn, wu_ref[:, c:c + f_chunk], preferred_element_type=F32)
        u = jnp.square(jnp.maximum(u, 0.0)).astype(BF16)
        out_ref[...] += jnp.dot(u, wd_ref[c:c + f_chunk, :], preferred_element_type=F32)


def out_mlp(h, o, w_out, norm_w, w_up, w_down):
    T, D = h.shape
    F = w_up.shape[1]
    tm = _row_tile(T)
    return pl.pallas_call(
        functools.partial(_out_mlp_body, f_chunk=1024),
        grid=(T // tm,),
        in_specs=[pl.BlockSpec((tm, D), lambda i: (i, 0)),
                  pl.BlockSpec((tm, D), lambda i: (i, 0)),
                  _resident((D, D)),
                  _resident((1, D)),
                  _resident((D, F)),
                  _resident((F, D))],
        out_specs=pl.BlockSpec((tm, D), lambda i: (i, 0)),
        out_shape=jax.ShapeDtypeStruct((T, D), F32),
        compiler_params=pltpu.CompilerParams(
            dimension_semantics=("parallel",), vmem_limit_bytes=VMEM_LIMIT),
        name="out_mlp",
    )(h, o, w_out, norm_w.reshape(1, D), w_up, w_down)


def _final_norm_body(h_ref, nw_ref, o_ref):
    o_ref[...] = _rms_scale(h_ref[...], nw_ref[...])


def final_norm(h, norm_w, seq):
    B, Lp, D = h.shape
    t = PAD
    return pl.pallas_call(
        _final_norm_body,
        grid=(B, seq // t),
        in_specs=[pl.BlockSpec((None, t, D), lambda b, i: (b, i + PAD // t, 0)),
                  _resident((1, D))],
        out_specs=pl.BlockSpec((None, t, D), lambda b, i: (b, i, 0)),
        out_shape=jax.ShapeDtypeStruct((B, seq, D), F32),
        compiler_params=pltpu.CompilerParams(dimension_semantics=("parallel", "parallel")),
        name="final_norm",
    )(h, norm_w.reshape(1, D))


def _lambda_init_for(layer_idx):
    return 0.8 - 0.6 * math.exp(-0.3 * layer_idx)


def kernel(x, meta_tokens, mix_norm_w, attn_w_in, attn_lambda, attn_subln_w, attn_w_out, gla_w_in, gla_w_gate_up, gla_gate_bias, gla_norm_w, gla_w_out, mlp_norm_w, mlp_w_up, mlp_w_down, final_norm_w):
    B, seq, D = x.shape
    Lp = PAD + seq
    T = B * Lp
    lead = jnp.concatenate([jnp.zeros((FIRST_KEY, D), x.dtype), meta_tokens.astype(x.dtype)], axis=0)
    h = jnp.concatenate([jnp.broadcast_to(lead[None], (B, PAD, D)), x], axis=1).reshape(T, D)

    for i in range(DEPTH):
        j = i // 2
        if i % 2 == 0:
            col_scale = jnp.where(jnp.arange(3 * D) < D, ATTN_Q_SCALE, 1.0).astype(F32)
            qkv = norm_proj(h, mix_norm_w[i], (attn_w_in[j] * col_scale).astype(BF16))
            o = diff_attention(qkv.reshape(B, Lp, 3 * D), attn_lambda[j], attn_subln_w[j],
                               _lambda_init_for(i))
            w_out = attn_w_out[j]
        else:
            n_main = 2 * GLA_KEY_DIM + 2 * GLA_VAL_DIM
            w_in = gla_w_in[j]
            w_gz = jnp.pad(w_in[:, n_main:], ((0, 0), (0, LANE - GLA_GATE_RANK))).astype(BF16)
            w_gu = jnp.pad(gla_w_gate_up[j], ((0, LANE - GLA_GATE_RANK), (0, 0))).astype(BF16)
            qkvg, glog = gla_proj(h, mix_norm_w[i], w_in[:, :n_main].astype(BF16), w_gz, w_gu,
                                  gla_gate_bias[j])
            o = gla_mixer(qkvg.reshape(B, Lp, n_main), glog.reshape(B, Lp, GLA_KEY_DIM), gla_norm_w[j])
            w_out = gla_w_out[j]
        h = out_mlp(h, o.reshape(T, D), w_out.astype(BF16), mlp_norm_w[i],
                    mlp_w_up[i].astype(BF16), mlp_w_down[i].astype(BF16))
    return final_norm(h.reshape(B, Lp, D), final_norm_w, seq)
```

```python
import functools
import math

import numpy as np
import jax
import jax.numpy as jnp
from jax import lax
from jax.experimental import pallas as pl
from jax.experimental.pallas import tpu as pltpu

F32 = jnp.float32
BF16 = jnp.bfloat16

D_MODEL = 1024
DEPTH = 4
N_META = 16
PAD = 256
FIRST_KEY = PAD - N_META
EPS = 1e-6
NEG = -1e30
LOG2E = math.log2(math.e)

DA_HEADS = 8
DA_HEAD_DIM = 64
DA_V_DIM = 128
GLA_HEADS = 4
GLA_HK = 128
GLA_HV = 256
GLA_KEY_DIM = 512
GLA_VAL_DIM = 1024
GLA_GATE_RANK = 16
GLA_GATE_NORM = 16.0
GLA_CHUNK = 128
GLA_LEVELS = 7
D_FF = 4096

LANE = 128
VMEM_LIMIT = 50 * 1024 * 1024

NT_DIMS = (((1,), (1,)), ((), ()))
TN_DIMS = (((0,), (0,)), ((), ()))


def _row_tile(n_rows):
    for t in (512, 384, 256, 128):
        if n_rows % t == 0:
            return t
    raise ValueError(f"row count {n_rows} is not a multiple of 128")


def _rms_scale(x, w):
    return x * lax.rsqrt(jnp.mean(x * x, axis=-1, keepdims=True) + EPS) * w


def _resident(shape):
    return pl.BlockSpec(shape, lambda *_: (0,) * len(shape), pipeline_mode=pl.Buffered(1))


def _norm_proj_body(h_ref, nw_ref, w_ref, qt_ref, kv_ref, *, n_chunk):
    hn = _rms_scale(h_ref[...], nw_ref[...]).astype(BF16)
    n_q = qt_ref.shape[0]
    for c in range(0, n_q, n_chunk):
        q = jnp.dot(hn, w_ref[:, c:c + n_chunk], preferred_element_type=F32)
        qt_ref[c:c + n_chunk, :] = q.T.astype(qt_ref.dtype)
    for c in range(n_q, w_ref.shape[1], n_chunk):
        kv_ref[:, c - n_q:c - n_q + n_chunk] = jnp.dot(
            hn, w_ref[:, c:c + n_chunk], preferred_element_type=F32).astype(kv_ref.dtype)


def norm_proj(h, norm_w, w):
    T, D = h.shape
    N = w.shape[1]
    tm = _row_tile(T)
    return pl.pallas_call(
        functools.partial(_norm_proj_body, n_chunk=512),
        grid=(T // tm,),
        in_specs=[pl.BlockSpec((tm, D), lambda i: (i, 0)),
                  _resident((1, D)),
                  _resident((D, N))],
        out_specs=[pl.BlockSpec((D, tm), lambda i: (0, i)),
                   pl.BlockSpec((tm, N - D), lambda i: (i, 0))],
        out_shape=[jax.ShapeDtypeStruct((D, T), BF16),
                   jax.ShapeDtypeStruct((T, N - D), BF16)],
        compiler_params=pltpu.CompilerParams(
            dimension_semantics=("parallel",), vmem_limit_bytes=VMEM_LIMIT),
        name="norm_proj",
    )(h, norm_w.reshape(1, D), w)


def _gla_proj_body(h_ref, nw_ref, w_ref, wgz_ref, wgu_ref, gb_ref, o_ref, gl_ref, *, n_chunk):
    hn = _rms_scale(h_ref[...], nw_ref[...]).astype(BF16)
    for c in range(0, w_ref.shape[1], n_chunk):
        o_ref[:, c:c + n_chunk] = jnp.dot(
            hn, w_ref[:, c:c + n_chunk], preferred_element_type=F32).astype(o_ref.dtype)
    gz = jnp.dot(hn, wgz_ref[...], preferred_element_type=F32).astype(BF16)
    z = jnp.dot(gz, wgu_ref[...], preferred_element_type=F32) + gb_ref[...]
    gl_ref[...] = (jnp.minimum(z, 0.0) - jnp.log1p(jnp.exp(-jnp.abs(z)))) * (LOG2E / GLA_GATE_NORM)


def gla_proj(h, norm_w, w_main, w_gz, w_gate_up, gate_bias):
    T, D = h.shape
    N = w_main.shape[1]
    tm = _row_tile(T)
    return pl.pallas_call(
        functools.partial(_gla_proj_body, n_chunk=512),
        grid=(T // tm,),
        in_specs=[pl.BlockSpec((tm, D), lambda i: (i, 0)),
                  _resident((1, D)),
                  _resident((D, N)),
                  _resident((D, LANE)),
                  _resident((LANE, GLA_KEY_DIM)),
                  _resident((1, GLA_KEY_DIM))],
        out_specs=[pl.BlockSpec((tm, N), lambda i: (i, 0)),
                   pl.BlockSpec((tm, GLA_KEY_DIM), lambda i: (i, 0))],
        out_shape=[jax.ShapeDtypeStruct((T, N), BF16),
                   jax.ShapeDtypeStruct((T, GLA_KEY_DIM), F32)],
        compiler_params=pltpu.CompilerParams(
            dimension_semantics=("parallel",), vmem_limit_bytes=VMEM_LIMIT),
        name="gla_proj",
    )(h, norm_w.reshape(1, D), w_main, w_gz, w_gate_up, gate_bias.reshape(1, GLA_KEY_DIM))


ATTN_TQ = 256
ATTN_HEADS_PER_STEP = 4
ATTN_ACC_ROWS = DA_V_DIM + 16
SLOPE_PIECES = 3
ATTN_Q_SCALE = DA_HEAD_DIM ** -0.5 * LOG2E


def _attn_body(slopes_ref, q_ref, k_ref, v_ref, lam_ref, sub_ref, o_ref, sa_ref, sb_ref, acc_ref, *,
               n_tiles, lambda_init):
    tq = ATTN_TQ
    hp = pl.program_id(1)
    qi = pl.program_id(2)
    row0 = pl.multiple_of(qi * tq, tq)
    lane = lax.broadcasted_iota(jnp.int32, (tq, LANE), 1)
    sub = lax.broadcasted_iota(jnp.int32, (tq, LANE), 0)
    pos = jnp.where(lane < SLOPE_PIECES, sub, 0).astype(BF16)
    ones_rows = jnp.ones((ATTN_ACC_ROWS - DA_V_DIM, tq), BF16)

    dim_i = lax.broadcasted_iota(jnp.int32, (DA_V_DIM, tq), 0)
    feat_i = lax.broadcasted_iota(jnp.int32, (LANE, 2 * tq), 0)
    heads = []
    for hh in range(ATTN_HEADS_PER_STEP):
        cs = slice(hh * DA_V_DIM, (hh + 1) * DA_V_DIM)
        h = ATTN_HEADS_PER_STEP * hp + hh
        q = q_ref[cs, :].astype(F32)
        feat = jnp.zeros((LANE, 2 * tq), F32)
        for j in range(SLOPE_PIECES):
            feat = jnp.where(feat_i == j, slopes_ref[h, j], feat)
        q_t = jnp.concatenate(
            [jnp.concatenate([jnp.where(dim_i < DA_HEAD_DIM, q, 0.0),
                              jnp.where(dim_i >= DA_HEAD_DIM, q, 0.0)], axis=1), feat], axis=0)
        heads.append((cs, slopes_ref[h, SLOPE_PIECES], q_t.astype(BF16)))

    def scores(head, start):
        k_aug = jnp.concatenate([k_ref[pl.ds(start, tq), head[0]], pos], axis=1)
        s = jnp.dot(k_aug, head[2], preferred_element_type=F32)
        return s, jnp.max(s, axis=0, keepdims=True)

    def update(hh, m, s, s_max, c, pv_and_sum):
        m_new = jnp.maximum(m, s_max + c)
        alpha = jnp.exp2(m - m_new)
        p = jnp.exp2(s - (m_new - c))
        acc_ref[hh] = alpha * acc_ref[hh] + pv_and_sum(p)
        return m_new

    def tile_start(t):
        return pl.multiple_of(t * tq, tq)

    def full_tile(hh, m, s, s_max, start):
        c = heads[hh][1] * (start - row0).astype(F32)
        v_t = jnp.concatenate([v_ref[pl.ds(start, tq), heads[hh][0]].T, ones_rows], axis=0)
        return update(hh, m, s, s_max, c,
                      lambda p: jnp.dot(v_t, p.astype(BF16), preferred_element_type=F32))

    n_full = qi - 1
    trips = jnp.maximum(n_full - 1, 0) // 2
    carries = []
    for hh, head in enumerate(heads):
        acc_ref[hh] = jnp.zeros((ATTN_ACC_ROWS, 2 * tq), F32)
        s, s_max = scores(head, tile_start(1))
        sa_ref[hh] = s
        carries.append((jnp.full((1, 2 * tq), NEG, F32), s_max))

    def issue(buf_ref, t):
        maxes = []
        for hh, head in enumerate(heads):
            s, s_max = scores(head, tile_start(t))
            buf_ref[hh] = s
            maxes.append(s_max)
        return maxes

    def consume(buf_ref, t, ms, maxes, off=None):
        out = []
        for hh, (m, s_max) in enumerate(zip(ms, maxes)):
            s = buf_ref[hh]
            if off is not None:
                s, s_max = s + off, s_max + off
            out.append(full_tile(hh, m, s, s_max, tile_start(t)))
        return out

    def body(i, carries):
        t = 1 + 2 * i
        max_b = issue(sb_ref, t + 1)
        ms = consume(sa_ref, t, [c[0] for c in carries], [c[1] for c in carries])
        max_a = issue(sa_ref, t + 2)
        ms = consume(sb_ref, t + 1, ms, max_b)
        return tuple(zip(ms, max_a))

    carries = lax.fori_loop(0, trips, body, tuple(carries))

    t_a = 1 + 2 * trips
    remaining = n_full - 2 * trips
    ms = consume(sa_ref, t_a, [c[0] for c in carries], [c[1] for c in carries],
                 jnp.where(remaining >= 1, 0.0, NEG))

    def second_tile(ms):
        return tuple(consume(sb_ref, t_a + 1, list(ms), issue(sb_ref, t_a + 1)))

    ms = lax.cond(remaining >= 2, second_tile, lambda ms: ms, tuple(ms))

    key_i = lax.broadcasted_iota(jnp.int32, (N_META + tq, 2 * tq), 0) - N_META
    qry_i = lax.broadcasted_iota(jnp.int32, (N_META + tq, 2 * tq), 1) & (tq - 1)
    meta_base = jnp.where(qi >= 1, PAD, -PAD)
    key_row = jnp.where(key_i < 0, meta_base + key_i, row0 + key_i)
    diag_mask = (key_row <= row0 + qry_i) & (key_row >= FIRST_KEY)
    pos_d = jnp.concatenate([pos[:N_META], pos], axis=0)
    lp = lam_ref[...]
    lam = (jnp.exp(jnp.sum(lp[0:1] * lp[1:2], axis=1, keepdims=True))
           - jnp.exp(jnp.sum(lp[2:3] * lp[3:4], axis=1, keepdims=True)) + lambda_init)
    row = row0 + lax.broadcasted_iota(jnp.int32, (tq, 1), 0)
    sum_rows = (ATTN_ACC_ROWS - DA_V_DIM, 2 * tq)
    for hh, (head, m) in enumerate(zip(heads, ms)):
        cs, slope, q_t = head
        k_d = jnp.concatenate([k_ref[pl.ds(FIRST_KEY, N_META), cs], k_ref[pl.ds(row0, tq), cs]], axis=0)
        v_d = jnp.concatenate([v_ref[pl.ds(FIRST_KEY, N_META), cs], v_ref[pl.ds(row0, tq), cs]], axis=0)
        s_d = jnp.dot(jnp.concatenate([k_d, pos_d], axis=1), q_t, preferred_element_type=F32)
        s_d = s_d + jnp.where(key_i[:, :1] < 0, slope * (FIRST_KEY - row0).astype(F32), 0.0)
        s_d = jnp.where(diag_mask, s_d, NEG)
        update(hh, m, s_d, jnp.max(s_d, axis=0, keepdims=True), 0.0,
               lambda p: jnp.concatenate(
                   [lax.dot_general(v_d, p.astype(BF16), TN_DIMS, preferred_element_type=F32),
                    jnp.broadcast_to(jnp.sum(p, axis=0, keepdims=True), sum_rows)], axis=0))
        acc = acc_ref[hh]
        o_t = acc[:DA_V_DIM] / acc[DA_V_DIM:DA_V_DIM + 1]
        d = (o_t[:, :tq] - lam * o_t[:, tq:]).T
        d = jnp.where(row >= FIRST_KEY, d, 0.0)
        y = _rms_scale(d, sub_ref[...]) * (1.0 - lambda_init)
        o_ref[:, cs] = y.astype(o_ref.dtype)


def diff_attention(q_t, kv, lam_params, subln_w, lambda_init):
    B, Lp, _ = kv.shape
    hps = ATTN_HEADS_PER_STEP
    w = hps * DA_V_DIM
    n_col = DA_HEADS // hps
    n_tiles = Lp // ATTN_TQ
    slope = 2.0 ** (-8.0 * jnp.arange(1, DA_HEADS + 1, dtype=F32) / DA_HEADS) * LOG2E
    pieces, rest = [], slope
    for _ in range(SLOPE_PIECES):
        pieces.append(rest.astype(BF16).astype(F32))
        rest = rest - pieces[-1]
    slopes = jnp.stack(pieces + [slope], axis=1)
    kv_spec = lambda sec: pl.BlockSpec((None, Lp, w), lambda b, h, i: (b, 0, sec * n_col + h))
    return pl.pallas_call(
        functools.partial(_attn_body, n_tiles=n_tiles, lambda_init=lambda_init),
        grid=(B, n_col, n_tiles),
        in_specs=[pl.BlockSpec(memory_space=pltpu.SMEM),
                  pl.BlockSpec((w, ATTN_TQ), lambda b, h, i: (h, b * n_tiles + i)),
                  kv_spec(0),
                  kv_spec(1),
                  _resident((4, DA_HEAD_DIM)),
                  _resident((1, DA_V_DIM))],
        out_specs=pl.BlockSpec((None, ATTN_TQ, w), lambda b, h, i: (b, i, h)),
        out_shape=jax.ShapeDtypeStruct((B, Lp, DA_HEADS * DA_V_DIM), BF16),
        scratch_shapes=[pltpu.VMEM((hps, ATTN_TQ, 2 * ATTN_TQ), F32),
                        pltpu.VMEM((hps, ATTN_TQ, 2 * ATTN_TQ), F32),
                        pltpu.VMEM((hps, ATTN_ACC_ROWS, 2 * ATTN_TQ), F32)],
        compiler_params=pltpu.CompilerParams(
            dimension_semantics=("parallel", "parallel", "arbitrary"), vmem_limit_bytes=VMEM_LIMIT),
        name="diff_attention",
    )(slopes, q_t, kv, kv, lam_params.astype(F32), subln_w.reshape(1, DA_V_DIM).astype(F32))


def _gla_tables():
    C = GLA_CHUNK
    i = np.arange(C)[:, None]
    t = np.arange(C)[None, :]
    blocks = [t <= i, t > i]
    level = np.full((C, C), GLA_LEVELS + 1, np.int32)
    level[np.arange(C), np.arange(C)] = GLA_LEVELS
    for l in range(GLA_LEVELS):
        s = C >> (l + 1)
        m = (i // (2 * s)) * 2 * s + s - 1
        upper = (i & s) != 0
        blocks.append(np.where(upper, (t > m) & (t <= i), (t > i) & (t <= m)))
        level[(i > t) & (((i ^ t) >> (GLA_LEVELS - 1 - l)) == 1)] = l
    sums = np.concatenate(blocks, axis=0).astype(np.float32)
    sums = np.concatenate([sums, sums], axis=1)
    return jnp.asarray(sums, BF16), jnp.asarray(level)


def _gla_body(level_ref, sums_ref, q_ref, k_ref, v_ref, g_ref, gl_ref, nw_ref, o_ref, st_ref):
    C = GLA_CHUNK

    @pl.when(pl.program_id(1) == 0)
    def _():
        st_ref[...] = jnp.zeros_like(st_ref)

    gl = gl_ref[...]
    g_hi = gl.astype(BF16)
    g_lo = (gl - g_hi.astype(F32)).astype(BF16)
    w_all = jnp.dot(sums_ref[...], jnp.concatenate([g_hi, g_lo], axis=0),
                    preferred_element_type=F32)
    level = level_ref[...]
    at_level = [level == l for l in range(GLA_LEVELS + 1)]

    for h in range(GLA_HEADS):
        ks = slice(h * GLA_HK, (h + 1) * GLA_HK)
        vs = slice(h * GLA_HV, (h + 1) * GLA_HV)
        q = q_ref[:, ks].astype(F32)
        k = k_ref[:, ks].astype(F32)
        v = v_ref[:, vs]
        w = w_all[:, ks]
        cum = w[0:C]
        a = jnp.where(at_level[GLA_LEVELS], jnp.sum(q * k, axis=1, keepdims=True), 0.0)
        for l in range(GLA_LEVELS):
            e = jnp.exp2(w[(2 + l) * C:(3 + l) * C])
            a_l = lax.dot_general((q * e).astype(BF16), (k * e).astype(BF16), NT_DIMS,
                                  preferred_element_type=F32)
            a = jnp.where(at_level[l], a_l, a)
        state_t = st_ref[h]
        qe = (q * jnp.exp2(cum)).astype(BF16)
        o = (jnp.dot(a.astype(BF16), v, preferred_element_type=F32)
             + lax.dot_general(qe, state_t.astype(BF16), NT_DIMS, preferred_element_type=F32))
        kd = (k * jnp.exp2(w[C:2 * C])).astype(BF16)
        st_ref[h] = (state_t * jnp.exp2(cum[C - 1:C])
                     + lax.dot_general(v, kd, TN_DIMS, preferred_element_type=F32))
        g = g_ref[:, vs].astype(F32)
        y = _rms_scale(o, nw_ref[...]) * (g / (1.0 + jnp.exp(-g)))
        o_ref[:, vs] = y.astype(o_ref.dtype)


def gla_mixer(qkvg, glog, norm_w):
    B, Lp, _ = qkvg.shape
    C = GLA_CHUNK
    sums, level = _gla_tables()
    kw, vw = GLA_KEY_DIM, GLA_VAL_DIM
    return pl.pallas_call(
        _gla_body,
        grid=(B, Lp // C),
        in_specs=[_resident((C, C)),
                  _resident(((2 + GLA_LEVELS) * C, 2 * C)),
                  pl.BlockSpec((None, C, kw), lambda b, c: (b, c, 0)),
                  pl.BlockSpec((None, C, kw), lambda b, c: (b, c, 1)),
                  pl.BlockSpec((None, C, vw), lambda b, c: (b, c, 1)),
                  pl.BlockSpec((None, C, vw), lambda b, c: (b, c, 2)),
                  pl.BlockSpec((None, C, kw), lambda b, c: (b, c, 0)),
                  _resident((1, GLA_HV))],
        out_specs=pl.BlockSpec((None, C, vw), lambda b, c: (b, c, 0)),
        out_shape=jax.ShapeDtypeStruct((B, Lp, vw), BF16),
        scratch_shapes=[pltpu.VMEM((GLA_HEADS, GLA_HV, GLA_HK), F32)],
        compiler_params=pltpu.CompilerParams(
            dimension_semantics=("parallel", "arbitrary"), vmem_limit_bytes=VMEM_LIMIT),
        name="gla_mixer",
    )(level, sums, qkvg, qkvg, qkvg, qkvg, glog, norm_w.reshape(1, GLA_HV).astype(F32))


def _out_mlp_body(h_ref, o_ref, wo_ref, nw_ref, wu_ref, wd_ref, out_ref, *, f_chunk):
    h1 = h_ref[...] + jnp.dot(o_ref[...], wo_ref[...], preferred_element_type=F32)
    hn = _rms_scale(h1, nw_ref[...]).astype(BF16)
    out_ref[...] = h1
    for c in range(0, wu_ref.shape[1], f_chunk):
        u = jnp.dot(hn, wu_ref[:, c:c + f_chunk], preferred_element_type=F32)
        u = jnp.square(jnp.maximum(u, 0.0)).astype(BF16)
        out_ref[...] += jnp.dot(u, wd_ref[c:c + f_chunk, :], preferred_element_type=F32)


def out_mlp(h, o, w_out, norm_w, w_up, w_down):
    T, D = h.shape
    F = w_up.shape[1]
    tm = _row_tile(T)
    return pl.pallas_call(
        functools.partial(_out_mlp_body, f_chunk=1024),
        grid=(T // tm,),
        in_specs=[pl.BlockSpec((tm, D), lambda i: (i, 0)),
                  pl.BlockSpec((tm, D), lambda i: (i, 0)),
                  _resident((D, D)),
                  _resident((1, D)),
                  _resident((D, F)),
                  _resident((F, D))],
        out_specs=pl.BlockSpec((tm, D), lambda i: (i, 0)),
        out_shape=jax.ShapeDtypeStruct((T, D), F32),
        compiler_params=pltpu.CompilerParams(
            dimension_semantics=("parallel",), vmem_limit_bytes=VMEM_LIMIT),
        name="out_mlp",
    )(h, o, w_out, norm_w.reshape(1, D), w_up, w_down)


def _final_norm_body(h_ref, nw_ref, o_ref):
    o_ref[...] = _rms_scale(h_ref[...], nw_ref[...])


def final_norm(h, norm_w, seq):
    B, Lp, D = h.shape
    t = PAD
    return pl.pallas_call(
        _final_norm_body,
        grid=(B, seq // t),
        in_specs=[pl.BlockSpec((None, t, D), lambda b, i: (b, i + PAD // t, 0)),
                  _resident((1, D))],
        out_specs=pl.BlockSpec((None, t, D), lambda b, i: (b, i, 0)),
        out_shape=jax.ShapeDtypeStruct((B, seq, D), F32),
        compiler_params=pltpu.CompilerParams(dimension_semantics=("parallel", "parallel")),
        name="final_norm",
    )(h, norm_w.reshape(1, D))


def _lambda_init_for(layer_idx):
    return 0.8 - 0.6 * math.exp(-0.3 * layer_idx)


def kernel(x, meta_tokens, mix_norm_w, attn_w_in, attn_lambda, attn_subln_w, attn_w_out, gla_w_in, gla_w_gate_up, gla_gate_bias, gla_norm_w, gla_w_out, mlp_norm_w, mlp_w_up, mlp_w_down, final_norm_w):
    B, seq, D = x.shape
    Lp = PAD + seq
    T = B * Lp
    lead = jnp.concatenate([jnp.zeros((FIRST_KEY, D), x.dtype), meta_tokens.astype(x.dtype)], axis=0)
    h = jnp.concatenate([jnp.broadcast_to(lead[None], (B, PAD, D)), x], axis=1).reshape(T, D)

    for i in range(DEPTH):
        j = i // 2
        if i % 2 == 0:
            col_scale = jnp.where(jnp.arange(3 * D) < D, ATTN_Q_SCALE, 1.0).astype(F32)
            q_t, kv = norm_proj(h, mix_norm_w[i], (attn_w_in[j] * col_scale).astype(BF16))
            o = diff_attention(q_t, kv.reshape(B, Lp, 2 * D), attn_lambda[j], attn_subln_w[j],
                               _lambda_init_for(i))
            w_out = attn_w_out[j]
        else:
            n_main = 2 * GLA_KEY_DIM + 2 * GLA_VAL_DIM
            w_in = gla_w_in[j]
            w_gz = jnp.pad(w_in[:, n_main:], ((0, 0), (0, LANE - GLA_GATE_RANK))).astype(BF16)
            w_gu = jnp.pad(gla_w_gate_up[j], ((0, LANE - GLA_GATE_RANK), (0, 0))).astype(BF16)
            q_scale = jnp.where(jnp.arange(n_main) < GLA_KEY_DIM, GLA_HK ** -0.5, 1.0).astype(F32)
            qkvg, glog = gla_proj(h, mix_norm_w[i], (w_in[:, :n_main] * q_scale).astype(BF16), w_gz, w_gu,
                                  gla_gate_bias[j])
            o = gla_mixer(qkvg.reshape(B, Lp, n_main), glog.reshape(B, Lp, GLA_KEY_DIM), gla_norm_w[j])
            w_out = gla_w_out[j]
        h = out_mlp(h, o.reshape(T, D), w_out.astype(BF16), mlp_norm_w[i],
                    mlp_w_up[i].astype(BF16), mlp_w_down[i].astype(BF16))
    return final_norm(h.reshape(B, Lp, D), final_norm_w, seq)
```

```python
import functools
import math

import numpy as np
import jax
import jax.numpy as jnp
from jax import lax
from jax.experimental import pallas as pl
from jax.experimental.pallas import tpu as pltpu

F32 = jnp.float32
BF16 = jnp.bfloat16

D_MODEL = 1024
DEPTH = 4
N_META = 16
PAD = 256
FIRST_KEY = PAD - N_META
EPS = 1e-6
NEG = -1e30
LOG2E = math.log2(math.e)

DA_HEADS = 8
DA_HEAD_DIM = 64
DA_V_DIM = 128
GLA_HEADS = 4
GLA_HK = 128
GLA_HV = 256
GLA_KEY_DIM = 512
GLA_VAL_DIM = 1024
GLA_GATE_RANK = 16
GLA_GATE_NORM = 16.0
GLA_CHUNK = 128
GLA_LEVELS = 7
GLA_CHUNKS_PER_STEP = 2
D_FF = 4096

LANE = 128
VMEM_LIMIT = 50 * 1024 * 1024

NT_DIMS = (((1,), (1,)), ((), ()))
TN_DIMS = (((0,), (0,)), ((), ()))


def _row_tile(n_rows):
    for t in (512, 384, 256, 128):
        if n_rows % t == 0:
            return t
    raise ValueError(f"row count {n_rows} is not a multiple of 128")


def _rms_scale(x, w):
    return x * lax.rsqrt(jnp.mean(x * x, axis=-1, keepdims=True) + EPS) * w


def _resident(shape):
    return pl.BlockSpec(shape, lambda *_: (0,) * len(shape), pipeline_mode=pl.Buffered(1))


def _norm_proj_body(h_ref, nw_ref, w_ref, qt_ref, kv_ref, *, n_chunk):
    hn = _rms_scale(h_ref[...], nw_ref[...]).astype(BF16)
    n_q = qt_ref.shape[0]
    for c in range(0, n_q, n_chunk):
        q = jnp.dot(hn, w_ref[:, c:c + n_chunk], preferred_element_type=F32)
        qt_ref[c:c + n_chunk, :] = q.T.astype(qt_ref.dtype)
    for c in range(n_q, w_ref.shape[1], n_chunk):
        kv_ref[:, c - n_q:c - n_q + n_chunk] = jnp.dot(
            hn, w_ref[:, c:c + n_chunk], preferred_element_type=F32).astype(kv_ref.dtype)


def norm_proj(h, norm_w, w):
    T, D = h.shape
    N = w.shape[1]
    tm = _row_tile(T)
    return pl.pallas_call(
        functools.partial(_norm_proj_body, n_chunk=512),
        grid=(T // tm,),
        in_specs=[pl.BlockSpec((tm, D), lambda i: (i, 0)),
                  _resident((1, D)),
                  _resident((D, N))],
        out_specs=[pl.BlockSpec((D, tm), lambda i: (0, i)),
                   pl.BlockSpec((tm, N - D), lambda i: (i, 0))],
        out_shape=[jax.ShapeDtypeStruct((D, T), BF16),
                   jax.ShapeDtypeStruct((T, N - D), BF16)],
        compiler_params=pltpu.CompilerParams(
            dimension_semantics=("parallel",), vmem_limit_bytes=VMEM_LIMIT),
        name="norm_proj",
    )(h, norm_w.reshape(1, D), w)


def _gla_proj_body(h_ref, nw_ref, w_ref, wgz_ref, wgu_ref, gb_ref, o_ref, gl_ref, *, n_chunk):
    hn = _rms_scale(h_ref[...], nw_ref[...]).astype(BF16)
    for c in range(0, w_ref.shape[1], n_chunk):
        o_ref[:, c:c + n_chunk] = jnp.dot(
            hn, w_ref[:, c:c + n_chunk], preferred_element_type=F32).astype(o_ref.dtype)
    gz = jnp.dot(hn, wgz_ref[...], preferred_element_type=F32).astype(BF16)
    z = jnp.dot(gz, wgu_ref[...], preferred_element_type=F32) + gb_ref[...]
    gl_ref[...] = (jnp.minimum(z, 0.0) - jnp.log1p(jnp.exp(-jnp.abs(z)))) * (LOG2E / GLA_GATE_NORM)


def gla_proj(h, norm_w, w_main, w_gz, w_gate_up, gate_bias):
    T, D = h.shape
    N = w_main.shape[1]
    tm = _row_tile(T)
    return pl.pallas_call(
        functools.partial(_gla_proj_body, n_chunk=512),
        grid=(T // tm,),
        in_specs=[pl.BlockSpec((tm, D), lambda i: (i, 0)),
                  _resident((1, D)),
                  _resident((D, N)),
                  _resident((D, LANE)),
                  _resident((LANE, GLA_KEY_DIM)),
                  _resident((1, GLA_KEY_DIM))],
        out_specs=[pl.BlockSpec((tm, N), lambda i: (i, 0)),
                   pl.BlockSpec((tm, GLA_KEY_DIM), lambda i: (i, 0))],
        out_shape=[jax.ShapeDtypeStruct((T, N), BF16),
                   jax.ShapeDtypeStruct((T, GLA_KEY_DIM), F32)],
        compiler_params=pltpu.CompilerParams(
            dimension_semantics=("parallel",), vmem_limit_bytes=VMEM_LIMIT),
        name="gla_proj",
    )(h, norm_w.reshape(1, D), w_main, w_gz, w_gate_up, gate_bias.reshape(1, GLA_KEY_DIM))


ATTN_TQ = 256
ATTN_HEADS_PER_STEP = 4
ATTN_ACC_ROWS = DA_V_DIM + 16
SLOPE_PIECES = 3
ATTN_Q_SCALE = DA_HEAD_DIM ** -0.5 * LOG2E


def _attn_body(slopes_ref, q_ref, k_ref, v_ref, lam_ref, sub_ref, o_ref, sa_ref, sb_ref, acc_ref, *,
               n_tiles, lambda_init):
    tq = ATTN_TQ
    hp = pl.program_id(1)
    qi = pl.program_id(2)
    row0 = pl.multiple_of(qi * tq, tq)
    lane = lax.broadcasted_iota(jnp.int32, (tq, LANE), 1)
    sub = lax.broadcasted_iota(jnp.int32, (tq, LANE), 0)
    pos = jnp.where(lane < SLOPE_PIECES, sub, 0).astype(BF16)
    ones_rows = jnp.ones((ATTN_ACC_ROWS - DA_V_DIM, tq), BF16)

    dim_i = lax.broadcasted_iota(jnp.int32, (DA_V_DIM, tq), 0)
    feat_i = lax.broadcasted_iota(jnp.int32, (LANE, 2 * tq), 0)
    heads = []
    for hh in range(ATTN_HEADS_PER_STEP):
        cs = slice(hh * DA_V_DIM, (hh + 1) * DA_V_DIM)
        h = ATTN_HEADS_PER_STEP * hp + hh
        q = q_ref[cs, :].astype(F32)
        feat = jnp.zeros((LANE, 2 * tq), F32)
        for j in range(SLOPE_PIECES):
            feat = jnp.where(feat_i == j, slopes_ref[h, j], feat)
        q_t = jnp.concatenate(
            [jnp.concatenate([jnp.where(dim_i < DA_HEAD_DIM, q, 0.0),
                              jnp.where(dim_i >= DA_HEAD_DIM, q, 0.0)], axis=1), feat], axis=0)
        heads.append((cs, slopes_ref[h, SLOPE_PIECES], q_t.astype(BF16)))

    def scores(head, start):
        k_aug = jnp.concatenate([k_ref[pl.ds(start, tq), head[0]], pos], axis=1)
        s = jnp.dot(k_aug, head[2], preferred_element_type=F32)
        return s, jnp.max(s, axis=0, keepdims=True)

    def update(hh, m, s, s_max, c, pv_and_sum):
        m_new = jnp.maximum(m, s_max + c)
        alpha = jnp.exp2(m - m_new)
        p = jnp.exp2(s - (m_new - c))
        acc_ref[hh] = alpha * acc_ref[hh] + pv_and_sum(p)
        return m_new

    def tile_start(t):
        return pl.multiple_of(t * tq, tq)

    def full_tile(hh, m, s, s_max, start):
        c = heads[hh][1] * (start - row0).astype(F32)
        v_t = jnp.concatenate([v_ref[pl.ds(start, tq), heads[hh][0]].T, ones_rows], axis=0)
        return update(hh, m, s, s_max, c,
                      lambda p: jnp.dot(v_t, p.astype(BF16), preferred_element_type=F32))

    n_full = qi - 1
    trips = jnp.maximum(n_full - 1, 0) // 2
    carries = []
    for hh, head in enumerate(heads):
        acc_ref[hh] = jnp.zeros((ATTN_ACC_ROWS, 2 * tq), F32)
        s, s_max = scores(head, tile_start(1))
        sa_ref[hh] = s
        carries.append((jnp.full((1, 2 * tq), NEG, F32), s_max))

    def issue(buf_ref, t):
        maxes = []
        for hh, head in enumerate(heads):
            s, s_max = scores(head, tile_start(t))
            buf_ref[hh] = s
            maxes.append(s_max)
        return maxes

    def consume(buf_ref, t, ms, maxes, off=None):
        out = []
        for hh, (m, s_max) in enumerate(zip(ms, maxes)):
            s = buf_ref[hh]
            if off is not None:
                s, s_max = s + off, s_max + off
            out.append(full_tile(hh, m, s, s_max, tile_start(t)))
        return out

    def body(i, carries):
        t = 1 + 2 * i
        max_b = issue(sb_ref, t + 1)
        ms = consume(sa_ref, t, [c[0] for c in carries], [c[1] for c in carries])
        max_a = issue(sa_ref, t + 2)
        ms = consume(sb_ref, t + 1, ms, max_b)
        return tuple(zip(ms, max_a))

    carries = lax.fori_loop(0, trips, body, tuple(carries))

    t_a = 1 + 2 * trips
    remaining = n_full - 2 * trips
    ms = consume(sa_ref, t_a, [c[0] for c in carries], [c[1] for c in carries],
                 jnp.where(remaining >= 1, 0.0, NEG))

    def second_tile(ms):
        return tuple(consume(sb_ref, t_a + 1, list(ms), issue(sb_ref, t_a + 1)))

    ms = lax.cond(remaining >= 2, second_tile, lambda ms: ms, tuple(ms))

    key_i = lax.broadcasted_iota(jnp.int32, (N_META + tq, 2 * tq), 0) - N_META
    qry_i = lax.broadcasted_iota(jnp.int32, (N_META + tq, 2 * tq), 1) & (tq - 1)
    meta_base = jnp.where(qi >= 1, PAD, -PAD)
    key_row = jnp.where(key_i < 0, meta_base + key_i, row0 + key_i)
    diag_mask = (key_row <= row0 + qry_i) & (key_row >= FIRST_KEY)
    pos_d = jnp.concatenate([pos[:N_META], pos], axis=0)
    lp = lam_ref[...]
    lam = (jnp.exp(jnp.sum(lp[0:1] * lp[1:2], axis=1, keepdims=True))
           - jnp.exp(jnp.sum(lp[2:3] * lp[3:4], axis=1, keepdims=True)) + lambda_init)
    row = row0 + lax.broadcasted_iota(jnp.int32, (tq, 1), 0)
    sum_rows = (ATTN_ACC_ROWS - DA_V_DIM, 2 * tq)
    for hh, (head, m) in enumerate(zip(heads, ms)):
        cs, slope, q_t = head
        k_d = jnp.concatenate([k_ref[pl.ds(FIRST_KEY, N_META), cs], k_ref[pl.ds(row0, tq), cs]], axis=0)
        v_d = jnp.concatenate([v_ref[pl.ds(FIRST_KEY, N_META), cs], v_ref[pl.ds(row0, tq), cs]], axis=0)
        s_d = jnp.dot(jnp.concatenate([k_d, pos_d], axis=1), q_t, preferred_element_type=F32)
        s_d = s_d + jnp.where(key_i[:, :1] < 0, slope * (FIRST_KEY - row0).astype(F32), 0.0)
        s_d = jnp.where(diag_mask, s_d, NEG)
        update(hh, m, s_d, jnp.max(s_d, axis=0, keepdims=True), 0.0,
               lambda p: jnp.concatenate(
                   [lax.dot_general(v_d, p.astype(BF16), TN_DIMS, preferred_element_type=F32),
                    jnp.broadcast_to(jnp.sum(p, axis=0, keepdims=True), sum_rows)], axis=0))
        acc = acc_ref[hh]
        o_t = acc[:DA_V_DIM] / acc[DA_V_DIM:DA_V_DIM + 1]
        d = (o_t[:, :tq] - lam * o_t[:, tq:]).T
        d = jnp.where(row >= FIRST_KEY, d, 0.0)
        y = _rms_scale(d, sub_ref[...]) * (1.0 - lambda_init)
        o_ref[:, cs] = y.astype(o_ref.dtype)


def diff_attention(q_t, kv, lam_params, subln_w, lambda_init):
    B, Lp, _ = kv.shape
    hps = ATTN_HEADS_PER_STEP
    w = hps * DA_V_DIM
    n_col = DA_HEADS // hps
    n_tiles = Lp // ATTN_TQ
    slope = 2.0 ** (-8.0 * jnp.arange(1, DA_HEADS + 1, dtype=F32) / DA_HEADS) * LOG2E
    pieces, rest = [], slope
    for _ in range(SLOPE_PIECES):
        pieces.append(rest.astype(BF16).astype(F32))
        rest = rest - pieces[-1]
    slopes = jnp.stack(pieces + [slope], axis=1)
    kv_spec = lambda sec: pl.BlockSpec((None, Lp, w), lambda b, h, i: (b, 0, sec * n_col + h))
    return pl.pallas_call(
        functools.partial(_attn_body, n_tiles=n_tiles, lambda_init=lambda_init),
        grid=(B, n_col, n_tiles),
        in_specs=[pl.BlockSpec(memory_space=pltpu.SMEM),
                  pl.BlockSpec((w, ATTN_TQ), lambda b, h, i: (h, b * n_tiles + i)),
                  kv_spec(0),
                  kv_spec(1),
                  _resident((4, DA_HEAD_DIM)),
                  _resident((1, DA_V_DIM))],
        out_specs=pl.BlockSpec((None, ATTN_TQ, w), lambda b, h, i: (b, i, h)),
        out_shape=jax.ShapeDtypeStruct((B, Lp, DA_HEADS * DA_V_DIM), BF16),
        scratch_shapes=[pltpu.VMEM((hps, ATTN_TQ, 2 * ATTN_TQ), F32),
                        pltpu.VMEM((hps, ATTN_TQ, 2 * ATTN_TQ), F32),
                        pltpu.VMEM((hps, ATTN_ACC_ROWS, 2 * ATTN_TQ), F32)],
        compiler_params=pltpu.CompilerParams(
            dimension_semantics=("parallel", "parallel", "arbitrary"), vmem_limit_bytes=VMEM_LIMIT),
        name="diff_attention",
    )(slopes, q_t, kv, kv, lam_params.astype(F32), subln_w.reshape(1, DA_V_DIM).astype(F32))


def _gla_tables():
    C = GLA_CHUNK
    i = np.arange(C)[:, None]
    t = np.arange(C)[None, :]
    blocks = [t <= i, t > i]
    level = np.full((C, C), GLA_LEVELS + 1, np.int32)
    level[np.arange(C), np.arange(C)] = GLA_LEVELS
    for l in range(GLA_LEVELS):
        s = C >> (l + 1)
        m = (i // (2 * s)) * 2 * s + s - 1
        upper = (i & s) != 0
        blocks.append(np.where(upper, (t > m) & (t <= i), (t > i) & (t <= m)))
        level[(i > t) & (((i ^ t) >> (GLA_LEVELS - 1 - l)) == 1)] = l
    sums = np.concatenate(blocks, axis=0).astype(np.float32)
    sums = np.concatenate([sums, sums], axis=1)
    return jnp.asarray(sums, BF16), jnp.asarray(level)


def _gla_body(level_ref, sums_ref, q_ref, k_ref, v_ref, g_ref, gl_ref, nw_ref, o_ref, st_ref):
    C = GLA_CHUNK

    @pl.when(pl.program_id(1) == 0)
    def _():
        st_ref[...] = jnp.zeros_like(st_ref)

    level = level_ref[...]
    at_level = [level == l for l in range(GLA_LEVELS + 1)]
    for cc in range(GLA_CHUNKS_PER_STEP):
        _gla_chunk(slice(cc * C, (cc + 1) * C), at_level, sums_ref, q_ref, k_ref, v_ref, g_ref, gl_ref,
                   nw_ref, o_ref, st_ref)


def _gla_chunk(rows, at_level, sums_ref, q_ref, k_ref, v_ref, g_ref, gl_ref, nw_ref, o_ref, st_ref):
    C = GLA_CHUNK
    gl = gl_ref[rows, :]
    g_hi = gl.astype(BF16)
    g_lo = (gl - g_hi.astype(F32)).astype(BF16)
    w_all = jnp.dot(sums_ref[...], jnp.concatenate([g_hi, g_lo], axis=0),
                    preferred_element_type=F32)

    for h in range(GLA_HEADS):
        ks = slice(h * GLA_HK, (h + 1) * GLA_HK)
        vs = slice(h * GLA_HV, (h + 1) * GLA_HV)
        q = q_ref[rows, ks].astype(F32)
        k = k_ref[rows, ks].astype(F32)
        v = v_ref[rows, vs]
        w = w_all[:, ks]
        cum = w[0:C]
        a = jnp.where(at_level[GLA_LEVELS], jnp.sum(q * k, axis=1, keepdims=True), 0.0)
        for l in range(GLA_LEVELS):
            e = jnp.exp2(w[(2 + l) * C:(3 + l) * C])
            a_l = lax.dot_general((q * e).astype(BF16), (k * e).astype(BF16), NT_DIMS,
                                  preferred_element_type=F32)
            a = jnp.where(at_level[l], a_l, a)
        state_t = st_ref[h]
        qe = (q * jnp.exp2(cum)).astype(BF16)
        o = (jnp.dot(a.astype(BF16), v, preferred_element_type=F32)
             + lax.dot_general(qe, state_t.astype(BF16), NT_DIMS, preferred_element_type=F32))
        kd = (k * jnp.exp2(w[C:2 * C])).astype(BF16)
        st_ref[h] = (state_t * jnp.exp2(cum[C - 1:C])
                     + lax.dot_general(v, kd, TN_DIMS, preferred_element_type=F32))
        g = g_ref[rows, vs].astype(F32)
        y = _rms_scale(o, nw_ref[...]) * (g / (1.0 + jnp.exp(-g)))
        o_ref[rows, vs] = y.astype(o_ref.dtype)


def gla_mixer(qkvg, glog, norm_w):
    B, Lp, _ = qkvg.shape
    C = GLA_CHUNK
    sums, level = _gla_tables()
    kw, vw = GLA_KEY_DIM, GLA_VAL_DIM
    R = GLA_CHUNKS_PER_STEP * C
    return pl.pallas_call(
        _gla_body,
        grid=(B, Lp // R),
        in_specs=[_resident((C, C)),
                  _resident(((2 + GLA_LEVELS) * C, 2 * C)),
                  pl.BlockSpec((None, R, kw), lambda b, c: (b, c, 0)),
                  pl.BlockSpec((None, R, kw), lambda b, c: (b, c, 1)),
                  pl.BlockSpec((None, R, vw), lambda b, c: (b, c, 1)),
                  pl.BlockSpec((None, R, vw), lambda b, c: (b, c, 2)),
                  pl.BlockSpec((None, R, kw), lambda b, c: (b, c, 0)),
                  _resident((1, GLA_HV))],
        out_specs=pl.BlockSpec((None, R, vw), lambda b, c: (b, c, 0)),
        out_shape=jax.ShapeDtypeStruct((B, Lp, vw), BF16),
        scratch_shapes=[pltpu.VMEM((GLA_HEADS, GLA_HV, GLA_HK), F32)],
        compiler_params=pltpu.CompilerParams(
            dimension_semantics=("parallel", "arbitrary"), vmem_limit_bytes=VMEM_LIMIT),
        name="gla_mixer",
    )(level, sums, qkvg, qkvg, qkvg, qkvg, glog, norm_w.reshape(1, GLA_HV).astype(F32))


def _out_mlp_body(h_ref, o_ref, wo_ref, nw_ref, wu_ref, wd_ref, *rest, f_chunk):
    out_ref = rest[-1]
    h1 = h_ref[...] + jnp.dot(o_ref[...], wo_ref[...], preferred_element_type=F32)
    hn = _rms_scale(h1, nw_ref[...]).astype(BF16)
    out_ref[...] = h1
    for c in range(0, wu_ref.shape[1], f_chunk):
        u = jnp.dot(hn, wu_ref[:, c:c + f_chunk], preferred_element_type=F32)
        u = jnp.square(jnp.maximum(u, 0.0)).astype(BF16)
        out_ref[...] += jnp.dot(u, wd_ref[c:c + f_chunk, :], preferred_element_type=F32)
    if len(rest) == 2:
        out_ref[...] = _rms_scale(out_ref[...], rest[0][...])


def out_mlp(h, o, w_out, norm_w, w_up, w_down):
    T, D = h.shape
    F = w_up.shape[1]
    tm = _row_tile(T)
    return pl.pallas_call(
        functools.partial(_out_mlp_body, f_chunk=1024),
        grid=(T // tm,),
        in_specs=[pl.BlockSpec((tm, D), lambda i: (i, 0)),
                  pl.BlockSpec((tm, D), lambda i: (i, 0)),
                  _resident((D, D)),
                  _resident((1, D)),
                  _resident((D, F)),
                  _resident((F, D))],
        out_specs=pl.BlockSpec((tm, D), lambda i: (i, 0)),
        out_shape=jax.ShapeDtypeStruct((T, D), F32),
        compiler_params=pltpu.CompilerParams(
            dimension_semantics=("parallel",), vmem_limit_bytes=VMEM_LIMIT),
        name="out_mlp",
    )(h, o, w_out, norm_w.reshape(1, D), w_up, w_down)


def out_mlp_final(h, o, w_out, norm_w, w_up, w_down, final_w):
    B, Lp, D = h.shape
    F = w_up.shape[1]
    t = PAD
    row_spec = pl.BlockSpec((None, t, D), lambda b, r: (b, r, 0))
    return pl.pallas_call(
        functools.partial(_out_mlp_body, f_chunk=1024),
        grid=(B, Lp // t),
        in_specs=[row_spec, row_spec,
                  _resident((D, D)),
                  _resident((1, D)),
                  _resident((D, F)),
                  _resident((F, D)),
                  _resident((1, D))],
        out_specs=pl.BlockSpec((None, t, D), lambda b, r: (b, jnp.maximum(r - 1, 0), 0)),
        out_shape=jax.ShapeDtypeStruct((B, Lp - PAD, D), F32),
        compiler_params=pltpu.CompilerParams(
            dimension_semantics=("parallel", "arbitrary"), vmem_limit_bytes=VMEM_LIMIT),
        name="out_mlp_final",
    )(h, o, w_out, norm_w.reshape(1, D), w_up, w_down, final_w.reshape(1, D))


def _lambda_init_for(layer_idx):
    return 0.8 - 0.6 * math.exp(-0.3 * layer_idx)


def kernel(x, meta_tokens, mix_norm_w, attn_w_in, attn_lambda, attn_subln_w, attn_w_out, gla_w_in, gla_w_gate_up, gla_gate_bias, gla_norm_w, gla_w_out, mlp_norm_w, mlp_w_up, mlp_w_down, final_norm_w):
    B, seq, D = x.shape
    Lp = PAD + seq
    T = B * Lp
    lead = jnp.concatenate([jnp.zeros((FIRST_KEY, D), x.dtype), meta_tokens.astype(x.dtype)], axis=0)
    h = jnp.concatenate([jnp.broadcast_to(lead[None], (B, PAD, D)), x], axis=1).reshape(T, D)

    for i in range(DEPTH):
        j = i // 2
        if i % 2 == 0:
            col_scale = jnp.where(jnp.arange(3 * D) < D, ATTN_Q_SCALE, 1.0).astype(F32)
            q_t, kv = norm_proj(h, mix_norm_w[i], (attn_w_in[j] * col_scale).astype(BF16))
            o = diff_attention(q_t, kv.reshape(B, Lp, 2 * D), attn_lambda[j], attn_subln_w[j],
                               _lambda_init_for(i))
            w_out = attn_w_out[j]
        else:
            n_main = 2 * GLA_KEY_DIM + 2 * GLA_VAL_DIM
            w_in = gla_w_in[j]
            w_gz = jnp.pad(w_in[:, n_main:], ((0, 0), (0, LANE - GLA_GATE_RANK))).astype(BF16)
            w_gu = jnp.pad(gla_w_gate_up[j], ((0, LANE - GLA_GATE_RANK), (0, 0))).astype(BF16)
            q_scale = jnp.where(jnp.arange(n_main) < GLA_KEY_DIM, GLA_HK ** -0.5, 1.0).astype(F32)
            qkvg, glog = gla_proj(h, mix_norm_w[i], (w_in[:, :n_main] * q_scale).astype(BF16), w_gz, w_gu,
                                  gla_gate_bias[j])
            o = gla_mixer(qkvg.reshape(B, Lp, n_main), glog.reshape(B, Lp, GLA_KEY_DIM), gla_norm_w[j])
            w_out = gla_w_out[j]
        mlp_args = (w_out.astype(BF16), mlp_norm_w[i], mlp_w_up[i].astype(BF16), mlp_w_down[i].astype(BF16))
        if i < DEPTH - 1:
            h = out_mlp(h, o.reshape(T, D), *mlp_args)
    return out_mlp_final(h.reshape(B, Lp, D), o.reshape(B, Lp, D), *mlp_args, final_norm_w)
```

```python
import functools
import math

import numpy as np
import jax
import jax.numpy as jnp
from jax import lax
from jax.experimental import pallas as pl
from jax.experimental.pallas import tpu as pltpu

F32 = jnp.float32
BF16 = jnp.bfloat16

D_MODEL = 1024
DEPTH = 4
N_META = 16
PAD = 256
FIRST_KEY = PAD - N_META
EPS = 1e-6
NEG = -1e30
LOG2E = math.log2(math.e)

DA_HEADS = 8
DA_HEAD_DIM = 64
DA_V_DIM = 128
GLA_HEADS = 4
GLA_HK = 128
GLA_HV = 256
GLA_KEY_DIM = 512
GLA_VAL_DIM = 1024
GLA_GATE_RANK = 16
GLA_GATE_NORM = 16.0
GLA_CHUNK = 128
GLA_LEVELS = 7
GLA_CHUNKS_PER_STEP = 2
D_FF = 4096

LANE = 128
VMEM_LIMIT = 50 * 1024 * 1024

NT_DIMS = (((1,), (1,)), ((), ()))
TN_DIMS = (((0,), (0,)), ((), ()))


def _row_tile(n_rows):
    for t in (512, 384, 256, 128):
        if n_rows % t == 0:
            return t
    raise ValueError(f"row count {n_rows} is not a multiple of 128")


def _rms_scale(x, w):
    return x * lax.rsqrt(jnp.mean(x * x, axis=-1, keepdims=True) + EPS) * w


def _resident(shape):
    return pl.BlockSpec(shape, lambda *_: (0,) * len(shape), pipeline_mode=pl.Buffered(1))


def _norm_proj_first_body(x_ref, lead_ref, nw_ref, w_ref, h_ref, qt_ref, kv_ref, *, n_chunk):
    @pl.when(pl.program_id(1) == 0)
    def _():
        h_ref[...] = lead_ref[...]

    @pl.when(pl.program_id(1) > 0)
    def _():
        h_ref[...] = x_ref[...]

    _norm_proj_body(h_ref, nw_ref, w_ref, qt_ref, kv_ref, n_chunk=n_chunk)


def norm_proj_first(x, lead, norm_w, w):
    B, seq, D = x.shape
    N = w.shape[1]
    t = PAD
    n = (PAD + seq) // t
    T = B * n * t
    return pl.pallas_call(
        functools.partial(_norm_proj_first_body, n_chunk=512),
        grid=(B, n),
        in_specs=[pl.BlockSpec((None, t, D), lambda b, r: (b, jnp.maximum(r - 1, 0), 0)),
                  _resident((t, D)),
                  _resident((1, D)),
                  _resident((D, N))],
        out_specs=[pl.BlockSpec((t, D), lambda b, r: (b * n + r, 0)),
                   pl.BlockSpec((D, t), lambda b, r: (0, b * n + r)),
                   pl.BlockSpec((t, N - D), lambda b, r: (b * n + r, 0))],
        out_shape=[jax.ShapeDtypeStruct((T, D), F32),
                   jax.ShapeDtypeStruct((D, T), BF16),
                   jax.ShapeDtypeStruct((T, N - D), BF16)],
        compiler_params=pltpu.CompilerParams(
            dimension_semantics=("parallel", "parallel"), vmem_limit_bytes=VMEM_LIMIT),
        name="norm_proj_first",
    )(x, lead, norm_w.reshape(1, D), w)


def _norm_proj_body(h_ref, nw_ref, w_ref, qt_ref, kv_ref, *, n_chunk):
    hn = _rms_scale(h_ref[...], nw_ref[...]).astype(BF16)
    n_q = qt_ref.shape[0]
    for c in range(0, n_q, n_chunk):
        q = jnp.dot(hn, w_ref[:, c:c + n_chunk], preferred_element_type=F32)
        qt_ref[c:c + n_chunk, :] = q.T.astype(qt_ref.dtype)
    for c in range(n_q, w_ref.shape[1], n_chunk):
        kv_ref[:, c - n_q:c - n_q + n_chunk] = jnp.dot(
            hn, w_ref[:, c:c + n_chunk], preferred_element_type=F32).astype(kv_ref.dtype)


def norm_proj(h, norm_w, w):
    T, D = h.shape
    N = w.shape[1]
    tm = _row_tile(T)
    return pl.pallas_call(
        functools.partial(_norm_proj_body, n_chunk=512),
        grid=(T // tm,),
        in_specs=[pl.BlockSpec((tm, D), lambda i: (i, 0)),
                  _resident((1, D)),
                  _resident((D, N))],
        out_specs=[pl.BlockSpec((D, tm), lambda i: (0, i)),
                   pl.BlockSpec((tm, N - D), lambda i: (i, 0))],
        out_shape=[jax.ShapeDtypeStruct((D, T), BF16),
                   jax.ShapeDtypeStruct((T, N - D), BF16)],
        compiler_params=pltpu.CompilerParams(
            dimension_semantics=("parallel",), vmem_limit_bytes=VMEM_LIMIT),
        name="norm_proj",
    )(h, norm_w.reshape(1, D), w)


def _gla_proj_body(h_ref, nw_ref, w_ref, wgz_ref, wgu_ref, gb_ref, o_ref, gl_ref, *, n_chunk):
    hn = _rms_scale(h_ref[...], nw_ref[...]).astype(BF16)
    for c in range(0, w_ref.shape[1], n_chunk):
        o_ref[:, c:c + n_chunk] = jnp.dot(
            hn, w_ref[:, c:c + n_chunk], preferred_element_type=F32).astype(o_ref.dtype)
    gz = jnp.dot(hn, wgz_ref[...], preferred_element_type=F32).astype(BF16)
    z = jnp.dot(gz, wgu_ref[...], preferred_element_type=F32) + gb_ref[...]
    gl_ref[...] = (jnp.minimum(z, 0.0) - jnp.log1p(jnp.exp(-jnp.abs(z)))) * (LOG2E / GLA_GATE_NORM)


def gla_proj(h, norm_w, w_main, w_gz, w_gate_up, gate_bias):
    T, D = h.shape
    N = w_main.shape[1]
    tm = _row_tile(T)
    return pl.pallas_call(
        functools.partial(_gla_proj_body, n_chunk=512),
        grid=(T // tm,),
        in_specs=[pl.BlockSpec((tm, D), lambda i: (i, 0)),
                  _resident((1, D)),
                  _resident((D, N)),
                  _resident((D, LANE)),
                  _resident((LANE, GLA_KEY_DIM)),
                  _resident((1, GLA_KEY_DIM))],
        out_specs=[pl.BlockSpec((tm, N), lambda i: (i, 0)),
                   pl.BlockSpec((tm, GLA_KEY_DIM), lambda i: (i, 0))],
        out_shape=[jax.ShapeDtypeStruct((T, N), BF16),
                   jax.ShapeDtypeStruct((T, GLA_KEY_DIM), F32)],
        compiler_params=pltpu.CompilerParams(
            dimension_semantics=("parallel",), vmem_limit_bytes=VMEM_LIMIT),
        name="gla_proj",
    )(h, norm_w.reshape(1, D), w_main, w_gz, w_gate_up, gate_bias.reshape(1, GLA_KEY_DIM))


ATTN_TQ = 256
ATTN_HEADS_PER_STEP = 4
ATTN_ACC_ROWS = DA_V_DIM + 16
SLOPE_PIECES = 3
ATTN_Q_SCALE = DA_HEAD_DIM ** -0.5 * LOG2E


def _attn_body(slopes_ref, q_ref, k_ref, v_ref, lam_ref, sub_ref, o_ref, sa_ref, sb_ref, acc_ref, *,
               n_tiles, lambda_init):
    tq = ATTN_TQ
    hp = pl.program_id(1)
    qi = pl.program_id(2)
    row0 = pl.multiple_of(qi * tq, tq)
    lane = lax.broadcasted_iota(jnp.int32, (tq, LANE), 1)
    sub = lax.broadcasted_iota(jnp.int32, (tq, LANE), 0)
    pos = jnp.where(lane < SLOPE_PIECES, sub, 0).astype(BF16)
    ones_rows = jnp.ones((ATTN_ACC_ROWS - DA_V_DIM, tq), BF16)

    dim_i = lax.broadcasted_iota(jnp.int32, (DA_V_DIM, tq), 0)
    feat_i = lax.broadcasted_iota(jnp.int32, (LANE, 2 * tq), 0)
    heads = []
    for hh in range(ATTN_HEADS_PER_STEP):
        cs = slice(hh * DA_V_DIM, (hh + 1) * DA_V_DIM)
        h = ATTN_HEADS_PER_STEP * hp + hh
        q = q_ref[cs, :].astype(F32)
        feat = jnp.zeros((LANE, 2 * tq), F32)
        for j in range(SLOPE_PIECES):
            feat = jnp.where(feat_i == j, slopes_ref[h, j], feat)
        q_t = jnp.concatenate(
            [jnp.concatenate([jnp.where(dim_i < DA_HEAD_DIM, q, 0.0),
                              jnp.where(dim_i >= DA_HEAD_DIM, q, 0.0)], axis=1), feat], axis=0)
        heads.append((cs, slopes_ref[h, SLOPE_PIECES], q_t.astype(BF16)))

    def scores(head, start):
        k_aug = jnp.concatenate([k_ref[pl.ds(start, tq), head[0]], pos], axis=1)
        s = jnp.dot(k_aug, head[2], preferred_element_type=F32)
        return s, jnp.max(s, axis=0, keepdims=True)

    def update(hh, m, s, s_max, c, pv_and_sum):
        m_new = jnp.maximum(m, s_max + c)
        alpha = jnp.exp2(m - m_new)
        p = jnp.exp2(s - (m_new - c))
        acc_ref[hh] = alpha * acc_ref[hh] + pv_and_sum(p)
        return m_new

    def tile_start(t):
        return pl.multiple_of(t * tq, tq)

    def full_tile(hh, m, s, s_max, start):
        c = heads[hh][1] * (start - row0).astype(F32)
        v_t = jnp.concatenate([v_ref[pl.ds(start, tq), heads[hh][0]].T, ones_rows], axis=0)
        return update(hh, m, s, s_max, c,
                      lambda p: jnp.dot(v_t, p.astype(BF16), preferred_element_type=F32))

    n_full = qi - 1
    trips = jnp.maximum(n_full - 1, 0) // 2
    carries = []
    for hh, head in enumerate(heads):
        acc_ref[hh] = jnp.zeros((ATTN_ACC_ROWS, 2 * tq), F32)
        s, s_max = scores(head, tile_start(1))
        sa_ref[hh] = s
        carries.append((jnp.full((1, 2 * tq), NEG, F32), s_max))

    def issue(buf_ref, t):
        maxes = []
        for hh, head in enumerate(heads):
            s, s_max = scores(head, tile_start(t))
            buf_ref[hh] = s
            maxes.append(s_max)
        return maxes

    def consume(buf_ref, t, ms, maxes, off=None):
        out = []
        for hh, (m, s_max) in enumerate(zip(ms, maxes)):
            s = buf_ref[hh]
            if off is not None:
                s, s_max = s + off, s_max + off
            out.append(full_tile(hh, m, s, s_max, tile_start(t)))
        return out

    def body(i, carries):
        t = 1 + 2 * i
        max_b = issue(sb_ref, t + 1)
        ms = consume(sa_ref, t, [c[0] for c in carries], [c[1] for c in carries])
        max_a = issue(sa_ref, t + 2)
        ms = consume(sb_ref, t + 1, ms, max_b)
        return tuple(zip(ms, max_a))

    carries = lax.fori_loop(0, trips, body, tuple(carries))

    t_a = 1 + 2 * trips
    remaining = n_full - 2 * trips
    ms = consume(sa_ref, t_a, [c[0] for c in carries], [c[1] for c in carries],
                 jnp.where(remaining >= 1, 0.0, NEG))

    def second_tile(ms):
        return tuple(consume(sb_ref, t_a + 1, list(ms), issue(sb_ref, t_a + 1)))

    ms = lax.cond(remaining >= 2, second_tile, lambda ms: ms, tuple(ms))

    key_i = lax.broadcasted_iota(jnp.int32, (N_META + tq, 2 * tq), 0) - N_META
    qry_i = lax.broadcasted_iota(jnp.int32, (N_META + tq, 2 * tq), 1) & (tq - 1)
    meta_base = jnp.where(qi >= 1, PAD, -PAD)
    key_row = jnp.where(key_i < 0, meta_base + key_i, row0 + key_i)
    diag_mask = (key_row <= row0 + qry_i) & (key_row >= FIRST_KEY)
    pos_d = jnp.concatenate([pos[:N_META], pos], axis=0)
    lp = lam_ref[...]
    lam = (jnp.exp(jnp.sum(lp[0:1] * lp[1:2], axis=1, keepdims=True))
           - jnp.exp(jnp.sum(lp[2:3] * lp[3:4], axis=1, keepdims=True)) + lambda_init)
    row = row0 + lax.broadcasted_iota(jnp.int32, (tq, 1), 0)
    sum_rows = (ATTN_ACC_ROWS - DA_V_DIM, 2 * tq)
    for hh, (head, m) in enumerate(zip(heads, ms)):
        cs, slope, q_t = head
        k_d = jnp.concatenate([k_ref[pl.ds(FIRST_KEY, N_META), cs], k_ref[pl.ds(row0, tq), cs]], axis=0)
        v_d = jnp.concatenate([v_ref[pl.ds(FIRST_KEY, N_META), cs], v_ref[pl.ds(row0, tq), cs]], axis=0)
        s_d = jnp.dot(jnp.concatenate([k_d, pos_d], axis=1), q_t, preferred_element_type=F32)
        s_d = s_d + jnp.where(key_i[:, :1] < 0, slope * (FIRST_KEY - row0).astype(F32), 0.0)
        s_d = jnp.where(diag_mask, s_d, NEG)
        update(hh, m, s_d, jnp.max(s_d, axis=0, keepdims=True), 0.0,
               lambda p: jnp.concatenate(
                   [lax.dot_general(v_d, p.astype(BF16), TN_DIMS, preferred_element_type=F32),
                    jnp.broadcast_to(jnp.sum(p, axis=0, keepdims=True), sum_rows)], axis=0))
        acc = acc_ref[hh]
        o_t = acc[:DA_V_DIM] * (1.0 / acc[DA_V_DIM:DA_V_DIM + 1])
        d = (o_t[:, :tq] - lam * o_t[:, tq:]).T
        d = jnp.where(row >= FIRST_KEY, d, 0.0)
        y = _rms_scale(d, sub_ref[...]) * (1.0 - lambda_init)
        o_ref[:, cs] = y.astype(o_ref.dtype)


def diff_attention(q_t, kv, lam_params, subln_w, lambda_init):
    B, Lp, _ = kv.shape
    hps = ATTN_HEADS_PER_STEP
    w = hps * DA_V_DIM
    n_col = DA_HEADS // hps
    n_tiles = Lp // ATTN_TQ
    slope = 2.0 ** (-8.0 * jnp.arange(1, DA_HEADS + 1, dtype=F32) / DA_HEADS) * LOG2E
    pieces, rest = [], slope
    for _ in range(SLOPE_PIECES):
        pieces.append(rest.astype(BF16).astype(F32))
        rest = rest - pieces[-1]
    slopes = jnp.stack(pieces + [slope], axis=1)
    kv_spec = lambda sec: pl.BlockSpec((None, Lp, w), lambda b, h, i: (b, 0, sec * n_col + h))
    return pl.pallas_call(
        functools.partial(_attn_body, n_tiles=n_tiles, lambda_init=lambda_init),
        grid=(B, n_col, n_tiles),
        in_specs=[pl.BlockSpec(memory_space=pltpu.SMEM),
                  pl.BlockSpec((w, ATTN_TQ), lambda b, h, i: (h, b * n_tiles + i)),
                  kv_spec(0),
                  kv_spec(1),
                  _resident((4, DA_HEAD_DIM)),
                  _resident((1, DA_V_DIM))],
        out_specs=pl.BlockSpec((None, ATTN_TQ, w), lambda b, h, i: (b, i, h)),
        out_shape=jax.ShapeDtypeStruct((B, Lp, DA_HEADS * DA_V_DIM), BF16),
        scratch_shapes=[pltpu.VMEM((hps, ATTN_TQ, 2 * ATTN_TQ), F32),
                        pltpu.VMEM((hps, ATTN_TQ, 2 * ATTN_TQ), F32),
                        pltpu.VMEM((hps, ATTN_ACC_ROWS, 2 * ATTN_TQ), F32)],
        compiler_params=pltpu.CompilerParams(
            dimension_semantics=("parallel", "parallel", "arbitrary"), vmem_limit_bytes=VMEM_LIMIT),
        name="diff_attention",
    )(slopes, q_t, kv, kv, lam_params.astype(F32), subln_w.reshape(1, DA_V_DIM).astype(F32))


def _gla_tables():
    C = GLA_CHUNK
    i = np.arange(C)[:, None]
    t = np.arange(C)[None, :]
    blocks = [t <= i, t > i]
    level = np.full((C, C), GLA_LEVELS + 1, np.int32)
    level[np.arange(C), np.arange(C)] = GLA_LEVELS
    for l in range(GLA_LEVELS):
        s = C >> (l + 1)
        m = (i // (2 * s)) * 2 * s + s - 1
        upper = (i & s) != 0
        blocks.append(np.where(upper, (t > m) & (t <= i), (t > i) & (t <= m)))
        level[(i > t) & (((i ^ t) >> (GLA_LEVELS - 1 - l)) == 1)] = l
    sums = np.concatenate(blocks, axis=0).astype(np.float32)
    sums = np.concatenate([sums, sums], axis=1)
    return jnp.asarray(sums, BF16), jnp.asarray(level)


def _gla_body(level_ref, sums_ref, q_ref, k_ref, v_ref, g_ref, gl_ref, nw_ref, o_ref, st_ref):
    C = GLA_CHUNK

    @pl.when(pl.program_id(1) == 0)
    def _():
        st_ref[...] = jnp.zeros_like(st_ref)

    level = level_ref[...]
    at_level = [level == l for l in range(GLA_LEVELS + 1)]
    for cc in range(GLA_CHUNKS_PER_STEP):
        _gla_chunk(slice(cc * C, (cc + 1) * C), at_level, sums_ref, q_ref, k_ref, v_ref, g_ref, gl_ref,
                   nw_ref, o_ref, st_ref)


def _gla_chunk(rows, at_level, sums_ref, q_ref, k_ref, v_ref, g_ref, gl_ref, nw_ref, o_ref, st_ref):
    C = GLA_CHUNK
    gl = gl_ref[rows, :]
    g_hi = gl.astype(BF16)
    g_lo = (gl - g_hi.astype(F32)).astype(BF16)
    w_all = jnp.dot(sums_ref[...], jnp.concatenate([g_hi, g_lo], axis=0),
                    preferred_element_type=F32)

    for h in range(GLA_HEADS):
        ks = slice(h * GLA_HK, (h + 1) * GLA_HK)
        vs = slice(h * GLA_HV, (h + 1) * GLA_HV)
        q = q_ref[rows, ks].astype(F32)
        k = k_ref[rows, ks].astype(F32)
        v = v_ref[rows, vs]
        w = w_all[:, ks]
        cum = w[0:C]
        a = jnp.where(at_level[GLA_LEVELS], jnp.sum(q * k, axis=1, keepdims=True), 0.0)
        for l in range(GLA_LEVELS):
            e = jnp.exp2(w[(2 + l) * C:(3 + l) * C])
            a_l = lax.dot_general((q * e).astype(BF16), (k * e).astype(BF16), NT_DIMS,
                                  preferred_element_type=F32)
            a = jnp.where(at_level[l], a_l, a)
        state_t = st_ref[h]
        qe = (q * jnp.exp2(cum)).astype(BF16)
        o = (jnp.dot(a.astype(BF16), v, preferred_element_type=F32)
             + lax.dot_general(qe, state_t.astype(BF16), NT_DIMS, preferred_element_type=F32))
        kd = (k * jnp.exp2(w[C:2 * C])).astype(BF16)
        st_ref[h] = (state_t * jnp.exp2(cum[C - 1:C])
                     + lax.dot_general(v, kd, TN_DIMS, preferred_element_type=F32))
        g = g_ref[rows, vs].astype(F32)
        y = _rms_scale(o, nw_ref[...]) * (g / (1.0 + jnp.exp(-g)))
        o_ref[rows, vs] = y.astype(o_ref.dtype)


def gla_mixer(qkvg, glog, norm_w):
    B, Lp, _ = qkvg.shape
    C = GLA_CHUNK
    sums, level = _gla_tables()
    kw, vw = GLA_KEY_DIM, GLA_VAL_DIM
    R = GLA_CHUNKS_PER_STEP * C
    return pl.pallas_call(
        _gla_body,
        grid=(B, Lp // R),
        in_specs=[_resident((C, C)),
                  _resident(((2 + GLA_LEVELS) * C, 2 * C)),
                  pl.BlockSpec((None, R, kw), lambda b, c: (b, c, 0)),
                  pl.BlockSpec((None, R, kw), lambda b, c: (b, c, 1)),
                  pl.BlockSpec((None, R, vw), lambda b, c: (b, c, 1)),
                  pl.BlockSpec((None, R, vw), lambda b, c: (b, c, 2)),
                  pl.BlockSpec((None, R, kw), lambda b, c: (b, c, 0)),
                  _resident((1, GLA_HV))],
        out_specs=pl.BlockSpec((None, R, vw), lambda b, c: (b, c, 0)),
        out_shape=jax.ShapeDtypeStruct((B, Lp, vw), BF16),
        scratch_shapes=[pltpu.VMEM((GLA_HEADS, GLA_HV, GLA_HK), F32)],
        compiler_params=pltpu.CompilerParams(
            dimension_semantics=("parallel", "arbitrary"), vmem_limit_bytes=VMEM_LIMIT),
        name="gla_mixer",
    )(level, sums, qkvg, qkvg, qkvg, qkvg, glog, norm_w.reshape(1, GLA_HV).astype(F32))


def _out_mlp_body(h_ref, o_ref, wo_ref, nw_ref, wu_ref, wd_ref, *rest, f_chunk):
    out_ref = rest[-1]
    h1 = h_ref[...] + jnp.dot(o_ref[...], wo_ref[...], preferred_element_type=F32)
    hn = _rms_scale(h1, nw_ref[...]).astype(BF16)
    out_ref[...] = h1
    for c in range(0, wu_ref.shape[1], f_chunk):
        u = jnp.dot(hn, wu_ref[:, c:c + f_chunk], preferred_element_type=F32)
        u = jnp.square(jnp.maximum(u, 0.0)).astype(BF16)
        out_ref[...] += jnp.dot(u, wd_ref[c:c + f_chunk, :], preferred_element_type=F32)
    if len(rest) == 2:
        out_ref[...] = _rms_scale(out_ref[...], rest[0][...])


def out_mlp(h, o, w_out, norm_w, w_up, w_down):
    T, D = h.shape
    F = w_up.shape[1]
    tm = _row_tile(T)
    return pl.pallas_call(
        functools.partial(_out_mlp_body, f_chunk=1024),
        grid=(T // tm,),
        in_specs=[pl.BlockSpec((tm, D), lambda i: (i, 0)),
                  pl.BlockSpec((tm, D), lambda i: (i, 0)),
                  _resident((D, D)),
                  _resident((1, D)),
                  _resident((D, F)),
                  _resident((F, D))],
        out_specs=pl.BlockSpec((tm, D), lambda i: (i, 0)),
        out_shape=jax.ShapeDtypeStruct((T, D), F32),
        compiler_params=pltpu.CompilerParams(
            dimension_semantics=("parallel",), vmem_limit_bytes=VMEM_LIMIT),
        name="out_mlp",
    )(h, o, w_out, norm_w.reshape(1, D), w_up, w_down)


def out_mlp_final(h, o, w_out, norm_w, w_up, w_down, final_w):
    B, Lp, D = h.shape
    F = w_up.shape[1]
    t = PAD
    row_spec = pl.BlockSpec((None, t, D), lambda b, r: (b, r, 0))
    return pl.pallas_call(
        functools.partial(_out_mlp_body, f_chunk=1024),
        grid=(B, Lp // t),
        in_specs=[row_spec, row_spec,
                  _resident((D, D)),
                  _resident((1, D)),
                  _resident((D, F)),
                  _resident((F, D)),
                  _resident((1, D))],
        out_specs=pl.BlockSpec((None, t, D), lambda b, r: (b, jnp.maximum(r - 1, 0), 0)),
        out_shape=jax.ShapeDtypeStruct((B, Lp - PAD, D), F32),
        compiler_params=pltpu.CompilerParams(
            dimension_semantics=("parallel", "arbitrary"), vmem_limit_bytes=VMEM_LIMIT),
        name="out_mlp_final",
    )(h, o, w_out, norm_w.reshape(1, D), w_up, w_down, final_w.reshape(1, D))


def _lambda_init_for(layer_idx):
    return 0.8 - 0.6 * math.exp(-0.3 * layer_idx)


def kernel(x, meta_tokens, mix_norm_w, attn_w_in, attn_lambda, attn_subln_w, attn_w_out, gla_w_in, gla_w_gate_up, gla_gate_bias, gla_norm_w, gla_w_out, mlp_norm_w, mlp_w_up, mlp_w_down, final_norm_w):
    B, seq, D = x.shape
    Lp = PAD + seq
    T = B * Lp
    lead = jnp.concatenate([jnp.zeros((FIRST_KEY, D), x.dtype), meta_tokens.astype(x.dtype)], axis=0)

    for i in range(DEPTH):
        j = i // 2
        if i % 2 == 0:
            col_scale = jnp.where(jnp.arange(3 * D) < D, ATTN_Q_SCALE, 1.0).astype(F32)
            w_in = (attn_w_in[j] * col_scale).astype(BF16)
            if i == 0:
                h, q_t, kv = norm_proj_first(x, lead, mix_norm_w[i], w_in)
            else:
                q_t, kv = norm_proj(h, mix_norm_w[i], w_in)
            o = diff_attention(q_t, kv.reshape(B, Lp, 2 * D), attn_lambda[j], attn_subln_w[j],
                               _lambda_init_for(i))
            w_out = attn_w_out[j]
        else:
            n_main = 2 * GLA_KEY_DIM + 2 * GLA_VAL_DIM
            w_in = gla_w_in[j]
            w_gz = jnp.pad(w_in[:, n_main:], ((0, 0), (0, LANE - GLA_GATE_RANK))).astype(BF16)
            w_gu = jnp.pad(gla_w_gate_up[j], ((0, LANE - GLA_GATE_RANK), (0, 0))).astype(BF16)
            q_scale = jnp.where(jnp.arange(n_main) < GLA_KEY_DIM, GLA_HK ** -0.5, 1.0).astype(F32)
            qkvg, glog = gla_proj(h, mix_norm_w[i], (w_in[:, :n_main] * q_scale).astype(BF16), w_gz, w_gu,
                                  gla_gate_bias[j])
            o = gla_mixer(qkvg.reshape(B, Lp, n_main), glog.reshape(B, Lp, GLA_KEY_DIM), gla_norm_w[j])
            w_out = gla_w_out[j]
        mlp_args = (w_out.astype(BF16), mlp_norm_w[i], mlp_w_up[i].astype(BF16), mlp_w_down[i].astype(BF16))
        if i < DEPTH - 1:
            h = out_mlp(h, o.reshape(T, D), *mlp_args)
    return out_mlp_final(h.reshape(B, Lp, D), o.reshape(B, Lp, D), *mlp_args, final_norm_w)
```

```python
import functools
import math

import numpy as np
import jax
import jax.numpy as jnp
from jax import lax
from jax.experimental import pallas as pl
from jax.experimental.pallas import tpu as pltpu

F32 = jnp.float32
BF16 = jnp.bfloat16

D_MODEL = 1024
DEPTH = 4
N_META = 16
PAD = 256
FIRST_KEY = PAD - N_META
EPS = 1e-6
NEG = -1e30
LOG2E = math.log2(math.e)

DA_HEADS = 8
DA_HEAD_DIM = 64
DA_V_DIM = 128
GLA_HEADS = 4
GLA_HK = 128
GLA_HV = 256
GLA_KEY_DIM = 512
GLA_VAL_DIM = 1024
GLA_GATE_RANK = 16
GLA_GATE_NORM = 16.0
GLA_CHUNK = 128
GLA_LEVELS = 7
GLA_CHUNKS_PER_STEP = 2
GLA_SEQS_PER_STEP = 2
D_FF = 4096

LANE = 128
VMEM_LIMIT = 50 * 1024 * 1024
ATTN_VMEM_LIMIT = 57 * 1024 * 1024

NT_DIMS = (((1,), (1,)), ((), ()))
TN_DIMS = (((0,), (0,)), ((), ()))


def _row_tile(n_rows):
    for t in (512, 384, 256, 128):
        if n_rows % t == 0:
            return t
    raise ValueError(f"row count {n_rows} is not a multiple of 128")


def _rms_scale(x, w):
    return x * lax.rsqrt(jnp.mean(x * x, axis=-1, keepdims=True) + EPS) * w


def _resident(shape):
    return pl.BlockSpec(shape, lambda *_: (0,) * len(shape), pipeline_mode=pl.Buffered(1))


def _norm_proj_first_body(x_ref, lead_ref, nw_ref, w_ref, h_ref, qt_ref, kv_ref, *, n_chunk):
    @pl.when(pl.program_id(1) == 0)
    def _():
        h_ref[...] = lead_ref[...]

    @pl.when(pl.program_id(1) > 0)
    def _():
        h_ref[...] = x_ref[...]

    _norm_proj_body(h_ref, nw_ref, w_ref, qt_ref, kv_ref, n_chunk=n_chunk)


def norm_proj_first(x, lead, norm_w, w):
    B, seq, D = x.shape
    N = w.shape[1]
    t = PAD
    n = (PAD + seq) // t
    T = B * n * t
    return pl.pallas_call(
        functools.partial(_norm_proj_first_body, n_chunk=512),
        grid=(B, n),
        in_specs=[pl.BlockSpec((None, t, D), lambda b, r: (b, jnp.maximum(r - 1, 0), 0)),
                  _resident((t, D)),
                  _resident((1, D)),
                  _resident((D, N))],
        out_specs=[pl.BlockSpec((t, D), lambda b, r: (b * n + r, 0)),
                   pl.BlockSpec((D, t), lambda b, r: (0, b * n + r)),
                   pl.BlockSpec((t, N - D), lambda b, r: (b * n + r, 0))],
        out_shape=[jax.ShapeDtypeStruct((T, D), F32),
                   jax.ShapeDtypeStruct((D, T), BF16),
                   jax.ShapeDtypeStruct((T, N - D), BF16)],
        compiler_params=pltpu.CompilerParams(
            dimension_semantics=("parallel", "parallel"), vmem_limit_bytes=VMEM_LIMIT),
        name="norm_proj_first",
    )(x, lead, norm_w.reshape(1, D), w)


def _norm_proj_body(h_ref, nw_ref, w_ref, qt_ref, kv_ref, *, n_chunk):
    hn = _rms_scale(h_ref[...], nw_ref[...]).astype(BF16)
    n_q = qt_ref.shape[0]
    for c in range(0, n_q, n_chunk):
        q = jnp.dot(hn, w_ref[:, c:c + n_chunk], preferred_element_type=F32)
        qt_ref[c:c + n_chunk, :] = q.T.astype(qt_ref.dtype)
    for c in range(n_q, w_ref.shape[1], n_chunk):
        kv_ref[:, c - n_q:c - n_q + n_chunk] = jnp.dot(
            hn, w_ref[:, c:c + n_chunk], preferred_element_type=F32).astype(kv_ref.dtype)


def norm_proj(h, norm_w, w):
    T, D = h.shape
    N = w.shape[1]
    tm = _row_tile(T)
    return pl.pallas_call(
        functools.partial(_norm_proj_body, n_chunk=512),
        grid=(T // tm,),
        in_specs=[pl.BlockSpec((tm, D), lambda i: (i, 0)),
                  _resident((1, D)),
                  _resident((D, N))],
        out_specs=[pl.BlockSpec((D, tm), lambda i: (0, i)),
                   pl.BlockSpec((tm, N - D), lambda i: (i, 0))],
        out_shape=[jax.ShapeDtypeStruct((D, T), BF16),
                   jax.ShapeDtypeStruct((T, N - D), BF16)],
        compiler_params=pltpu.CompilerParams(
            dimension_semantics=("parallel",), vmem_limit_bytes=VMEM_LIMIT),
        name="norm_proj",
    )(h, norm_w.reshape(1, D), w)


def _gla_proj_body(h_ref, nw_ref, w_ref, wgz_ref, wgu_ref, gb_ref, o_ref, gl_ref, *, n_chunk):
    hn = _rms_scale(h_ref[...], nw_ref[...]).astype(BF16)
    for c in range(0, w_ref.shape[1], n_chunk):
        o_ref[:, c:c + n_chunk] = jnp.dot(
            hn, w_ref[:, c:c + n_chunk], preferred_element_type=F32).astype(o_ref.dtype)
    gz = jnp.dot(hn, wgz_ref[...], preferred_element_type=F32).astype(BF16)
    z = jnp.dot(gz, wgu_ref[...], preferred_element_type=F32) + gb_ref[...]
    gl_ref[...] = (jnp.minimum(z, 0.0) - jnp.log1p(jnp.exp(-jnp.abs(z)))) * (LOG2E / GLA_GATE_NORM)


def gla_proj(h, norm_w, w_main, w_gz, w_gate_up, gate_bias):
    T, D = h.shape
    N = w_main.shape[1]
    tm = _row_tile(T)
    return pl.pallas_call(
        functools.partial(_gla_proj_body, n_chunk=512),
        grid=(T // tm,),
        in_specs=[pl.BlockSpec((tm, D), lambda i: (i, 0)),
                  _resident((1, D)),
                  _resident((D, N)),
                  _resident((D, LANE)),
                  _resident((LANE, GLA_KEY_DIM)),
                  _resident((1, GLA_KEY_DIM))],
        out_specs=[pl.BlockSpec((tm, N), lambda i: (i, 0)),
                   pl.BlockSpec((tm, GLA_KEY_DIM), lambda i: (i, 0))],
        out_shape=[jax.ShapeDtypeStruct((T, N), BF16),
                   jax.ShapeDtypeStruct((T, GLA_KEY_DIM), F32)],
        compiler_params=pltpu.CompilerParams(
            dimension_semantics=("parallel",), vmem_limit_bytes=VMEM_LIMIT),
        name="gla_proj",
    )(h, norm_w.reshape(1, D), w_main, w_gz, w_gate_up, gate_bias.reshape(1, GLA_KEY_DIM))


ATTN_TQ = 256
ATTN_HEADS_PER_STEP = 8
ATTN_ACC_ROWS = DA_V_DIM + 16
SLOPE_PIECES = 3
ATTN_Q_SCALE = DA_HEAD_DIM ** -0.5 * LOG2E


def _attn_body(slopes_ref, q_ref, k_ref, v_ref, lam_ref, sub_ref, o_ref, sa_ref, sb_ref, acc_ref, *,
               n_tiles, lambda_init):
    tq = ATTN_TQ
    hp = pl.program_id(1)
    qi = pl.program_id(2)
    row0 = pl.multiple_of(qi * tq, tq)
    lane = lax.broadcasted_iota(jnp.int32, (tq, LANE), 1)
    sub = lax.broadcasted_iota(jnp.int32, (tq, LANE), 0)
    pos = jnp.where(lane < SLOPE_PIECES, sub, 0).astype(BF16)
    ones_rows = jnp.ones((ATTN_ACC_ROWS - DA_V_DIM, tq), BF16)

    dim_i = lax.broadcasted_iota(jnp.int32, (DA_V_DIM, tq), 0)
    feat_i = lax.broadcasted_iota(jnp.int32, (LANE, 2 * tq), 0)
    heads = []
    for hh in range(ATTN_HEADS_PER_STEP):
        cs = slice(hh * DA_V_DIM, (hh + 1) * DA_V_DIM)
        h = ATTN_HEADS_PER_STEP * hp + hh
        q = q_ref[cs, :].astype(F32)
        feat = jnp.zeros((LANE, 2 * tq), F32)
        for j in range(SLOPE_PIECES):
            feat = jnp.where(feat_i == j, slopes_ref[h, j], feat)
        q_t = jnp.concatenate(
            [jnp.concatenate([jnp.where(dim_i < DA_HEAD_DIM, q, 0.0),
                              jnp.where(dim_i >= DA_HEAD_DIM, q, 0.0)], axis=1), feat], axis=0)
        heads.append((cs, slopes_ref[h, SLOPE_PIECES], q_t.astype(BF16)))

    def scores(head, start):
        k_aug = jnp.concatenate([k_ref[pl.ds(start, tq), head[0]], pos], axis=1)
        s = jnp.dot(k_aug, head[2], preferred_element_type=F32)
        return s, jnp.max(s, axis=0, keepdims=True)

    def update(hh, m, s, s_max, c, pv_and_sum):
        m_new = jnp.maximum(m, s_max + c)
        alpha = jnp.exp2(m - m_new)
        p = jnp.exp2(s - (m_new - c))
        acc_ref[hh] = alpha * acc_ref[hh] + pv_and_sum(p)
        return m_new

    def tile_start(t):
        return pl.multiple_of(t * tq, tq)

    def full_tile(hh, m, s, s_max, start):
        c = heads[hh][1] * (start - row0).astype(F32)
        v_t = jnp.concatenate([v_ref[pl.ds(start, tq), heads[hh][0]].T, ones_rows], axis=0)
        return update(hh, m, s, s_max, c,
                      lambda p: jnp.dot(v_t, p.astype(BF16), preferred_element_type=F32))

    n_full = qi - 1
    trips = jnp.maximum(n_full - 1, 0) // 2
    carries = []
    for hh, head in enumerate(heads):
        acc_ref[hh] = jnp.zeros((ATTN_ACC_ROWS, 2 * tq), F32)
        s, s_max = scores(head, tile_start(1))
        sa_ref[hh] = s
        carries.append((jnp.full((1, 2 * tq), NEG, F32), s_max))

    def issue(buf_ref, t):
        maxes = []
        for hh, head in enumerate(heads):
            s, s_max = scores(head, tile_start(t))
            buf_ref[hh] = s
            maxes.append(s_max)
        return maxes

    def consume(buf_ref, t, ms, maxes, off=None):
        out = []
        for hh, (m, s_max) in enumerate(zip(ms, maxes)):
            s = buf_ref[hh]
            if off is not None:
                s, s_max = s + off, s_max + off
            out.append(full_tile(hh, m, s, s_max, tile_start(t)))
        return out

    def body(i, carries):
        t = 1 + 2 * i
        max_b = issue(sb_ref, t + 1)
        ms = consume(sa_ref, t, [c[0] for c in carries], [c[1] for c in carries])
        max_a = issue(sa_ref, t + 2)
        ms = consume(sb_ref, t + 1, ms, max_b)
        return tuple(zip(ms, max_a))

    carries = lax.fori_loop(0, trips, body, tuple(carries))

    t_a = 1 + 2 * trips
    remaining = n_full - 2 * trips
    ms = consume(sa_ref, t_a, [c[0] for c in carries], [c[1] for c in carries],
                 jnp.where(remaining >= 1, 0.0, NEG))

    def second_tile(ms):
        return tuple(consume(sb_ref, t_a + 1, list(ms), issue(sb_ref, t_a + 1)))

    ms = lax.cond(remaining >= 2, second_tile, lambda ms: ms, tuple(ms))

    key_i = lax.broadcasted_iota(jnp.int32, (N_META + tq, 2 * tq), 0) - N_META
    qry_i = lax.broadcasted_iota(jnp.int32, (N_META + tq, 2 * tq), 1) & (tq - 1)
    meta_base = jnp.where(qi >= 1, PAD, -PAD)
    key_row = jnp.where(key_i < 0, meta_base + key_i, row0 + key_i)
    diag_mask = (key_row <= row0 + qry_i) & (key_row >= FIRST_KEY)
    pos_d = jnp.concatenate([pos[:N_META], pos], axis=0)
    lp = lam_ref[...]
    lam = (jnp.exp(jnp.sum(lp[0:1] * lp[1:2], axis=1, keepdims=True))
           - jnp.exp(jnp.sum(lp[2:3] * lp[3:4], axis=1, keepdims=True)) + lambda_init)
    row = row0 + lax.broadcasted_iota(jnp.int32, (tq, 1), 0)
    sum_rows = (ATTN_ACC_ROWS - DA_V_DIM, 2 * tq)
    for hh, (head, m) in enumerate(zip(heads, ms)):
        cs, slope, q_t = head
        k_d = jnp.concatenate([k_ref[pl.ds(FIRST_KEY, N_META), cs], k_ref[pl.ds(row0, tq), cs]], axis=0)
        v_d = jnp.concatenate([v_ref[pl.ds(FIRST_KEY, N_META), cs], v_ref[pl.ds(row0, tq), cs]], axis=0)
        s_d = jnp.dot(jnp.concatenate([k_d, pos_d], axis=1), q_t, preferred_element_type=F32)
        s_d = s_d + jnp.where(key_i[:, :1] < 0, slope * (FIRST_KEY - row0).astype(F32), 0.0)
        s_d = jnp.where(diag_mask, s_d, NEG)
        update(hh, m, s_d, jnp.max(s_d, axis=0, keepdims=True), 0.0,
               lambda p: jnp.concatenate(
                   [lax.dot_general(v_d, p.astype(BF16), TN_DIMS, preferred_element_type=F32),
                    jnp.broadcast_to(jnp.sum(p, axis=0, keepdims=True), sum_rows)], axis=0))
        acc = acc_ref[hh]
        o_t = acc[:DA_V_DIM] * (1.0 / acc[DA_V_DIM:DA_V_DIM + 1])
        d = (o_t[:, :tq] - lam * o_t[:, tq:]).T
        d = jnp.where(row >= FIRST_KEY, d, 0.0)
        y = _rms_scale(d, sub_ref[...]) * (1.0 - lambda_init)
        o_ref[:, cs] = y.astype(o_ref.dtype)


def diff_attention(q_t, kv, lam_params, subln_w, lambda_init):
    B, Lp, _ = kv.shape
    hps = ATTN_HEADS_PER_STEP
    w = hps * DA_V_DIM
    n_col = DA_HEADS // hps
    n_tiles = Lp // ATTN_TQ
    slope = 2.0 ** (-8.0 * jnp.arange(1, DA_HEADS + 1, dtype=F32) / DA_HEADS) * LOG2E
    pieces, rest = [], slope
    for _ in range(SLOPE_PIECES):
        pieces.append(rest.astype(BF16).astype(F32))
        rest = rest - pieces[-1]
    slopes = jnp.stack(pieces + [slope], axis=1)
    kv_spec = lambda sec: pl.BlockSpec((None, Lp, w), lambda b, h, i: (b, 0, sec * n_col + h))
    return pl.pallas_call(
        functools.partial(_attn_body, n_tiles=n_tiles, lambda_init=lambda_init),
        grid=(B, n_col, n_tiles),
        in_specs=[pl.BlockSpec(memory_space=pltpu.SMEM),
                  pl.BlockSpec((w, ATTN_TQ), lambda b, h, i: (h, b * n_tiles + i)),
                  kv_spec(0),
                  kv_spec(1),
                  _resident((4, DA_HEAD_DIM)),
                  _resident((1, DA_V_DIM))],
        out_specs=pl.BlockSpec((None, ATTN_TQ, w), lambda b, h, i: (b, i, h)),
        out_shape=jax.ShapeDtypeStruct((B, Lp, DA_HEADS * DA_V_DIM), BF16),
        scratch_shapes=[pltpu.VMEM((hps, ATTN_TQ, 2 * ATTN_TQ), F32),
                        pltpu.VMEM((hps, ATTN_TQ, 2 * ATTN_TQ), F32),
                        pltpu.VMEM((hps, ATTN_ACC_ROWS, 2 * ATTN_TQ), F32)],
        compiler_params=pltpu.CompilerParams(
            dimension_semantics=("parallel", "parallel", "arbitrary"), vmem_limit_bytes=ATTN_VMEM_LIMIT),
        name="diff_attention",
    )(slopes, q_t, kv, kv, lam_params.astype(F32), subln_w.reshape(1, DA_V_DIM).astype(F32))


def _gla_tables():
    C = GLA_CHUNK
    i = np.arange(C)[:, None]
    t = np.arange(C)[None, :]
    blocks = [t <= i, t > i]
    level = np.full((C, C), GLA_LEVELS + 1, np.int32)
    level[np.arange(C), np.arange(C)] = GLA_LEVELS
    for l in range(GLA_LEVELS):
        s = C >> (l + 1)
        m = (i // (2 * s)) * 2 * s + s - 1
        upper = (i & s) != 0
        blocks.append(np.where(upper, (t > m) & (t <= i), (t > i) & (t <= m)))
        level[(i > t) & (((i ^ t) >> (GLA_LEVELS - 1 - l)) == 1)] = l
    sums = np.concatenate(blocks, axis=0).astype(np.float32)
    sums = np.concatenate([sums, sums], axis=1)
    return jnp.asarray(sums, BF16), jnp.asarray(level)


def _gla_body(level_ref, sums_ref, q_ref, k_ref, v_ref, g_ref, gl_ref, nw_ref, o_ref, st_ref, *, n_seqs):
    C = GLA_CHUNK

    @pl.when(pl.program_id(1) == 0)
    def _():
        st_ref[...] = jnp.zeros_like(st_ref)

    level = level_ref[...]
    at_level = [level == l for l in range(GLA_LEVELS + 1)]
    for cc in range(GLA_CHUNKS_PER_STEP):
        for seq in range(n_seqs):
            _gla_chunk(seq, slice(cc * C, (cc + 1) * C), at_level, sums_ref, q_ref, k_ref, v_ref, g_ref,
                       gl_ref, nw_ref, o_ref, st_ref)


def _gla_chunk(seq, rows, at_level, sums_ref, q_ref, k_ref, v_ref, g_ref, gl_ref, nw_ref, o_ref, st_ref):
    C = GLA_CHUNK
    gl = gl_ref[seq, rows, :]
    g_hi = gl.astype(BF16)
    g_lo = (gl - g_hi.astype(F32)).astype(BF16)
    w_all = jnp.dot(sums_ref[...], jnp.concatenate([g_hi, g_lo], axis=0),
                    preferred_element_type=F32)

    for h in range(GLA_HEADS):
        ks = slice(h * GLA_HK, (h + 1) * GLA_HK)
        vs = slice(h * GLA_HV, (h + 1) * GLA_HV)
        q = q_ref[seq, rows, ks].astype(F32)
        k = k_ref[seq, rows, ks].astype(F32)
        v = v_ref[seq, rows, vs]
        w = w_all[:, ks]
        cum = w[0:C]
        a = jnp.where(at_level[GLA_LEVELS], jnp.sum(q * k, axis=1, keepdims=True), 0.0)
        for l in range(GLA_LEVELS):
            e = jnp.exp2(w[(2 + l) * C:(3 + l) * C])
            a_l = lax.dot_general((q * e).astype(BF16), (k * e).astype(BF16), NT_DIMS,
                                  preferred_element_type=F32)
            a = jnp.where(at_level[l], a_l, a)
        state_t = st_ref[seq, h]
        qe = (q * jnp.exp2(cum)).astype(BF16)
        o = (jnp.dot(a.astype(BF16), v, preferred_element_type=F32)
             + lax.dot_general(qe, state_t.astype(BF16), NT_DIMS, preferred_element_type=F32))
        kd = (k * jnp.exp2(w[C:2 * C])).astype(BF16)
        st_ref[seq, h] = (state_t * jnp.exp2(cum[C - 1:C])
                     + lax.dot_general(v, kd, TN_DIMS, preferred_element_type=F32))
        g = g_ref[seq, rows, vs].astype(F32)
        y = _rms_scale(o, nw_ref[...]) * (g / (1.0 + jnp.exp(-g)))
        o_ref[seq, rows, vs] = y.astype(o_ref.dtype)


def gla_mixer(qkvg, glog, norm_w):
    B, Lp, _ = qkvg.shape
    C = GLA_CHUNK
    sums, level = _gla_tables()
    kw, vw = GLA_KEY_DIM, GLA_VAL_DIM
    R = GLA_CHUNKS_PER_STEP * C
    S = GLA_SEQS_PER_STEP if B % GLA_SEQS_PER_STEP == 0 else 1
    return pl.pallas_call(
        functools.partial(_gla_body, n_seqs=S),
        grid=(B // S, Lp // R),
        in_specs=[_resident((C, C)),
                  _resident(((2 + GLA_LEVELS) * C, 2 * C)),
                  pl.BlockSpec((S, R, kw), lambda b, c: (b, c, 0)),
                  pl.BlockSpec((S, R, kw), lambda b, c: (b, c, 1)),
                  pl.BlockSpec((S, R, vw), lambda b, c: (b, c, 1)),
                  pl.BlockSpec((S, R, vw), lambda b, c: (b, c, 2)),
                  pl.BlockSpec((S, R, kw), lambda b, c: (b, c, 0)),
                  _resident((1, GLA_HV))],
        out_specs=pl.BlockSpec((S, R, vw), lambda b, c: (b, c, 0)),
        out_shape=jax.ShapeDtypeStruct((B, Lp, vw), BF16),
        scratch_shapes=[pltpu.VMEM((S, GLA_HEADS, GLA_HV, GLA_HK), F32)],
        compiler_params=pltpu.CompilerParams(
            dimension_semantics=("parallel", "arbitrary"), vmem_limit_bytes=VMEM_LIMIT),
        name="gla_mixer",
    )(level, sums, qkvg, qkvg, qkvg, qkvg, glog, norm_w.reshape(1, GLA_HV).astype(F32))


def _out_mlp_body(h_ref, o_ref, wo_ref, nw_ref, wu_ref, wd_ref, *rest, f_chunk):
    out_ref = rest[-1]
    h1 = h_ref[...] + jnp.dot(o_ref[...], wo_ref[...], preferred_element_type=F32)
    hn = _rms_scale(h1, nw_ref[...]).astype(BF16)
    out_ref[...] = h1
    for c in range(0, wu_ref.shape[1], f_chunk):
        u = jnp.dot(hn, wu_ref[:, c:c + f_chunk], preferred_element_type=F32)
        u = jnp.square(jnp.maximum(u, 0.0)).astype(BF16)
        out_ref[...] += jnp.dot(u, wd_ref[c:c + f_chunk, :], preferred_element_type=F32)
    if len(rest) == 2:
        out_ref[...] = _rms_scale(out_ref[...], rest[0][...])


def out_mlp(h, o, w_out, norm_w, w_up, w_down):
    T, D = h.shape
    F = w_up.shape[1]
    tm = _row_tile(T)
    return pl.pallas_call(
        functools.partial(_out_mlp_body, f_chunk=1024),
        grid=(T // tm,),
        in_specs=[pl.BlockSpec((tm, D), lambda i: (i, 0)),
                  pl.BlockSpec((tm, D), lambda i: (i, 0)),
                  _resident((D, D)),
                  _resident((1, D)),
                  _resident((D, F)),
                  _resident((F, D))],
        out_specs=pl.BlockSpec((tm, D), lambda i: (i, 0)),
        out_shape=jax.ShapeDtypeStruct((T, D), F32),
        compiler_params=pltpu.CompilerParams(
            dimension_semantics=("parallel",), vmem_limit_bytes=VMEM_LIMIT),
        name="out_mlp",
    )(h, o, w_out, norm_w.reshape(1, D), w_up, w_down)


def out_mlp_final(h, o, w_out, norm_w, w_up, w_down, final_w):
    B, Lp, D = h.shape
    F = w_up.shape[1]
    t = PAD
    row_spec = pl.BlockSpec((None, t, D), lambda b, r: (b, r, 0))
    return pl.pallas_call(
        functools.partial(_out_mlp_body, f_chunk=1024),
        grid=(B, Lp // t),
        in_specs=[row_spec, row_spec,
                  _resident((D, D)),
                  _resident((1, D)),
                  _resident((D, F)),
                  _resident((F, D)),
                  _resident((1, D))],
        out_specs=pl.BlockSpec((None, t, D), lambda b, r: (b, jnp.maximum(r - 1, 0), 0)),
        out_shape=jax.ShapeDtypeStruct((B, Lp - PAD, D), F32),
        compiler_params=pltpu.CompilerParams(
            dimension_semantics=("parallel", "arbitrary"), vmem_limit_bytes=VMEM_LIMIT),
        name="out_mlp_final",
    )(h, o, w_out, norm_w.reshape(1, D), w_up, w_down, final_w.reshape(1, D))


def _lambda_init_for(layer_idx):
    return 0.8 - 0.6 * math.exp(-0.3 * layer_idx)


def kernel(x, meta_tokens, mix_norm_w, attn_w_in, attn_lambda, attn_subln_w, attn_w_out, gla_w_in, gla_w_gate_up, gla_gate_bias, gla_norm_w, gla_w_out, mlp_norm_w, mlp_w_up, mlp_w_down, final_norm_w):
    B, seq, D = x.shape
    Lp = PAD + seq
    T = B * Lp
    lead = jnp.concatenate([jnp.zeros((FIRST_KEY, D), x.dtype), meta_tokens.astype(x.dtype)], axis=0)

    for i in range(DEPTH):
        j = i // 2
        if i % 2 == 0:
            col_scale = jnp.where(jnp.arange(3 * D) < D, ATTN_Q_SCALE, 1.0).astype(F32)
            w_in = (attn_w_in[j] * col_scale).astype(BF16)
            if i == 0:
                h, q_t, kv = norm_proj_first(x, lead, mix_norm_w[i], w_in)
            else:
                q_t, kv = norm_proj(h, mix_norm_w[i], w_in)
            o = diff_attention(q_t, kv.reshape(B, Lp, 2 * D), attn_lambda[j], attn_subln_w[j],
                               _lambda_init_for(i))
            w_out = attn_w_out[j]
        else:
            n_main = 2 * GLA_KEY_DIM + 2 * GLA_VAL_DIM
            w_in = gla_w_in[j]
            w_gz = jnp.pad(w_in[:, n_main:], ((0, 0), (0, LANE - GLA_GATE_RANK))).astype(BF16)
            w_gu = jnp.pad(gla_w_gate_up[j], ((0, LANE - GLA_GATE_RANK), (0, 0))).astype(BF16)
            q_scale = jnp.where(jnp.arange(n_main) < GLA_KEY_DIM, GLA_HK ** -0.5, 1.0).astype(F32)
            qkvg, glog = gla_proj(h, mix_norm_w[i], (w_in[:, :n_main] * q_scale).astype(BF16), w_gz, w_gu,
                                  gla_gate_bias[j])
            o = gla_mixer(qkvg.reshape(B, Lp, n_main), glog.reshape(B, Lp, GLA_KEY_DIM), gla_norm_w[j])
            w_out = gla_w_out[j]
        mlp_args = (w_out.astype(BF16), mlp_norm_w[i], mlp_w_up[i].astype(BF16), mlp_w_down[i].astype(BF16))
        if i < DEPTH - 1:
            h = out_mlp(h, o.reshape(T, D), *mlp_args)
    return out_mlp_final(h.reshape(B, Lp, D), o.reshape(B, Lp, D), *mlp_args, final_norm_w)
```

```python
import functools
import math

import numpy as np
import jax
import jax.numpy as jnp
from jax import lax
from jax.experimental import pallas as pl
from jax.experimental.pallas import tpu as pltpu

F32 = jnp.float32
BF16 = jnp.bfloat16

DEPTH = 4
N_META = 16
ATTN_TQ = 256
PAD = ATTN_TQ
FIRST_KEY = PAD - N_META
EPS = 1e-6
NEG = -1e30
LOG2E = math.log2(math.e)

DA_HEADS = 8
DA_HEAD_DIM = 64
DA_V_DIM = 128
GLA_HEADS = 4
GLA_HK = 128
GLA_HV = 256
GLA_KEY_DIM = 512
GLA_VAL_DIM = 1024
GLA_GATE_RANK = 16
GLA_GATE_NORM = 16.0
GLA_CHUNK = 128
GLA_LEVELS = 7
GLA_CHUNKS_PER_STEP = 2
GLA_SEQS_PER_STEP = 2
PROJ_N_CHUNK = 512
MLP_F_CHUNK = 1024

LANE = 128
VMEM_LIMIT = 50 * 1024 * 1024
ATTN_VMEM_LIMIT = 57 * 1024 * 1024

NT_DIMS = (((1,), (1,)), ((), ()))
TN_DIMS = (((0,), (0,)), ((), ()))


def _row_tile(n_rows):
    for t in (512, 384, 256, 128):
        if n_rows % t == 0:
            return t
    raise ValueError(f"row count {n_rows} is not a multiple of 128")


def _rms_scale(x, w):
    return x * lax.rsqrt(jnp.mean(x * x, axis=-1, keepdims=True) + EPS) * w


def _resident(shape):
    return pl.BlockSpec(shape, lambda *_: (0,) * len(shape), pipeline_mode=pl.Buffered(1))


def _norm_proj_first_body(x_ref, lead_ref, nw_ref, w_ref, h_ref, qt_ref, kv_ref, *, n_chunk):
    @pl.when(pl.program_id(1) == 0)
    def _():
        h_ref[...] = lead_ref[...]

    @pl.when(pl.program_id(1) > 0)
    def _():
        h_ref[...] = x_ref[...]

    _norm_proj_body(h_ref, nw_ref, w_ref, qt_ref, kv_ref, n_chunk=n_chunk)


def norm_proj_first(x, lead, norm_w, w):
    B, seq, D = x.shape
    N = w.shape[1]
    t = PAD
    n = (PAD + seq) // t
    T = B * n * t
    return pl.pallas_call(
        functools.partial(_norm_proj_first_body, n_chunk=PROJ_N_CHUNK),
        grid=(B, n),
        in_specs=[pl.BlockSpec((None, t, D), lambda b, r: (b, jnp.maximum(r - 1, 0), 0)),
                  _resident((t, D)),
                  _resident((1, D)),
                  _resident((D, N))],
        out_specs=[pl.BlockSpec((t, D), lambda b, r: (b * n + r, 0)),
                   pl.BlockSpec((D, t), lambda b, r: (0, b * n + r)),
                   pl.BlockSpec((t, N - D), lambda b, r: (b * n + r, 0))],
        out_shape=[jax.ShapeDtypeStruct((T, D), F32),
                   jax.ShapeDtypeStruct((D, T), BF16),
                   jax.ShapeDtypeStruct((T, N - D), BF16)],
        compiler_params=pltpu.CompilerParams(
            dimension_semantics=("parallel", "parallel"), vmem_limit_bytes=VMEM_LIMIT),
        name="norm_proj_first",
    )(x, lead, norm_w.reshape(1, D), w)


def _norm_proj_body(h_ref, nw_ref, w_ref, qt_ref, kv_ref, *, n_chunk):
    hn = _rms_scale(h_ref[...], nw_ref[...]).astype(BF16)
    n_q = qt_ref.shape[0]
    for c in range(0, n_q, n_chunk):
        q = jnp.dot(hn, w_ref[:, c:c + n_chunk], preferred_element_type=F32)
        qt_ref[c:c + n_chunk, :] = q.T.astype(qt_ref.dtype)
    for c in range(n_q, w_ref.shape[1], n_chunk):
        kv_ref[:, c - n_q:c - n_q + n_chunk] = jnp.dot(
            hn, w_ref[:, c:c + n_chunk], preferred_element_type=F32).astype(kv_ref.dtype)


def norm_proj(h, norm_w, w):
    T, D = h.shape
    N = w.shape[1]
    tm = _row_tile(T)
    return pl.pallas_call(
        functools.partial(_norm_proj_body, n_chunk=PROJ_N_CHUNK),
        grid=(T // tm,),
        in_specs=[pl.BlockSpec((tm, D), lambda i: (i, 0)),
                  _resident((1, D)),
                  _resident((D, N))],
        out_specs=[pl.BlockSpec((D, tm), lambda i: (0, i)),
                   pl.BlockSpec((tm, N - D), lambda i: (i, 0))],
        out_shape=[jax.ShapeDtypeStruct((D, T), BF16),
                   jax.ShapeDtypeStruct((T, N - D), BF16)],
        compiler_params=pltpu.CompilerParams(
            dimension_semantics=("parallel",), vmem_limit_bytes=VMEM_LIMIT),
        name="norm_proj",
    )(h, norm_w.reshape(1, D), w)


def _gla_proj_body(h_ref, nw_ref, w_ref, wgz_ref, wgu_ref, gb_ref, o_ref, gl_ref, *, n_chunk):
    hn = _rms_scale(h_ref[...], nw_ref[...]).astype(BF16)
    for c in range(0, w_ref.shape[1], n_chunk):
        o_ref[:, c:c + n_chunk] = jnp.dot(
            hn, w_ref[:, c:c + n_chunk], preferred_element_type=F32).astype(o_ref.dtype)
    gz = jnp.dot(hn, wgz_ref[...], preferred_element_type=F32).astype(BF16)
    z = jnp.dot(gz, wgu_ref[...], preferred_element_type=F32) + gb_ref[...]
    gl_ref[...] = (jnp.minimum(z, 0.0) - jnp.log1p(jnp.exp(-jnp.abs(z)))) * (LOG2E / GLA_GATE_NORM)


def gla_proj(h, norm_w, w_main, w_gz, w_gate_up, gate_bias):
    T, D = h.shape
    N = w_main.shape[1]
    tm = _row_tile(T)
    return pl.pallas_call(
        functools.partial(_gla_proj_body, n_chunk=PROJ_N_CHUNK),
        grid=(T // tm,),
        in_specs=[pl.BlockSpec((tm, D), lambda i: (i, 0)),
                  _resident((1, D)),
                  _resident((D, N)),
                  _resident((D, LANE)),
                  _resident((LANE, GLA_KEY_DIM)),
                  _resident((1, GLA_KEY_DIM))],
        out_specs=[pl.BlockSpec((tm, N), lambda i: (i, 0)),
                   pl.BlockSpec((tm, GLA_KEY_DIM), lambda i: (i, 0))],
        out_shape=[jax.ShapeDtypeStruct((T, N), BF16),
                   jax.ShapeDtypeStruct((T, GLA_KEY_DIM), F32)],
        compiler_params=pltpu.CompilerParams(
            dimension_semantics=("parallel",), vmem_limit_bytes=VMEM_LIMIT),
        name="gla_proj",
    )(h, norm_w.reshape(1, D), w_main, w_gz, w_gate_up, gate_bias.reshape(1, GLA_KEY_DIM))


ATTN_HEADS_PER_STEP = 8
ATTN_ACC_ROWS = DA_V_DIM + 16
SLOPE_PIECES = 3
ATTN_Q_SCALE = DA_HEAD_DIM ** -0.5 * LOG2E


def _attn_body(slopes_ref, q_ref, k_ref, v_ref, lam_ref, sub_ref, o_ref, sa_ref, sb_ref, acc_ref, *,
               n_tiles, lambda_init):
    tq = ATTN_TQ
    hp = pl.program_id(1)
    qi = pl.program_id(2)
    row0 = pl.multiple_of(qi * tq, tq)
    lane = lax.broadcasted_iota(jnp.int32, (tq, LANE), 1)
    sub = lax.broadcasted_iota(jnp.int32, (tq, LANE), 0)
    pos = jnp.where(lane < SLOPE_PIECES, sub, 0).astype(BF16)
    ones_rows = jnp.ones((ATTN_ACC_ROWS - DA_V_DIM, tq), BF16)

    dim_i = lax.broadcasted_iota(jnp.int32, (DA_V_DIM, tq), 0)
    feat_i = lax.broadcasted_iota(jnp.int32, (LANE, 2 * tq), 0)
    heads = []
    for hh in range(ATTN_HEADS_PER_STEP):
        cs = slice(hh * DA_V_DIM, (hh + 1) * DA_V_DIM)
        h = ATTN_HEADS_PER_STEP * hp + hh
        q = q_ref[cs, :].astype(F32)
        feat = jnp.zeros((LANE, 2 * tq), F32)
        for j in range(SLOPE_PIECES):
            feat = jnp.where(feat_i == j, slopes_ref[h, j], feat)
        q_t = jnp.concatenate(
            [jnp.concatenate([jnp.where(dim_i < DA_HEAD_DIM, q, 0.0),
                              jnp.where(dim_i >= DA_HEAD_DIM, q, 0.0)], axis=1), feat], axis=0)
        heads.append((cs, slopes_ref[h, SLOPE_PIECES], q_t.astype(BF16)))

    def scores(head, start):
        k_aug = jnp.concatenate([k_ref[pl.ds(start, tq), head[0]], pos], axis=1)
        s = jnp.dot(k_aug, head[2], preferred_element_type=F32)
        return s, jnp.max(s, axis=0, keepdims=True)

    def update(hh, m, s, s_max, c, pv_and_sum):
        m_new = jnp.maximum(m, s_max + c)
        alpha = jnp.exp2(m - m_new)
        p = jnp.exp2(s - (m_new - c))
        acc_ref[hh] = alpha * acc_ref[hh] + pv_and_sum(p)
        return m_new

    def tile_start(t):
        return pl.multiple_of(t * tq, tq)

    def full_tile(hh, m, s, s_max, start):
        c = heads[hh][1] * (start - row0).astype(F32)
        v_t = jnp.concatenate([v_ref[pl.ds(start, tq), heads[hh][0]].T, ones_rows], axis=0)
        return update(hh, m, s, s_max, c,
                      lambda p: jnp.dot(v_t, p.astype(BF16), preferred_element_type=F32))

    n_full = qi - 1
    trips = jnp.maximum(n_full - 1, 0) // 2
    carries = []
    for hh, head in enumerate(heads):
        acc_ref[hh] = jnp.zeros((ATTN_ACC_ROWS, 2 * tq), F32)
        s, s_max = scores(head, tile_start(1))
        sa_ref[hh] = s
        carries.append((jnp.full((1, 2 * tq), NEG, F32), s_max))

    def issue(buf_ref, t):
        maxes = []
        for hh, head in enumerate(heads):
            s, s_max = scores(head, tile_start(t))
            buf_ref[hh] = s
            maxes.append(s_max)
        return maxes

    def consume(buf_ref, t, ms, maxes, off=None):
        out = []
        for hh, (m, s_max) in enumerate(zip(ms, maxes)):
            s = buf_ref[hh]
            if off is not None:
                s, s_max = s + off, s_max + off
            out.append(full_tile(hh, m, s, s_max, tile_start(t)))
        return out

    def body(i, carries):
        t = 1 + 2 * i
        max_b = issue(sb_ref, t + 1)
        ms = consume(sa_ref, t, [c[0] for c in carries], [c[1] for c in carries])
        max_a = issue(sa_ref, t + 2)
        ms = consume(sb_ref, t + 1, ms, max_b)
        return tuple(zip(ms, max_a))

    carries = lax.fori_loop(0, trips, body, tuple(carries))

    t_a = 1 + 2 * trips
    remaining = n_full - 2 * trips
    ms = consume(sa_ref, t_a, [c[0] for c in carries], [c[1] for c in carries],
                 jnp.where(remaining >= 1, 0.0, NEG))

    def second_tile(ms):
        return tuple(consume(sb_ref, t_a + 1, list(ms), issue(sb_ref, t_a + 1)))

    ms = lax.cond(remaining >= 2, second_tile, lambda ms: ms, tuple(ms))

    half = tq // 2
    key_i = lax.broadcasted_iota(jnp.int32, (N_META + half, 2 * tq), 0) - N_META
    qry_i = lax.broadcasted_iota(jnp.int32, (N_META + half, 2 * tq), 1) & (tq - 1)
    meta_base = jnp.where(qi >= 1, PAD, -PAD)
    key_row = jnp.where(key_i < 0, meta_base + key_i, row0 + key_i)
    mask_1 = (key_row <= row0 + qry_i) & (key_row >= FIRST_KEY)
    key_2 = half + lax.broadcasted_iota(jnp.int32, (half, tq), 0)
    qry_2 = half + (lax.broadcasted_iota(jnp.int32, (half, tq), 1) & (half - 1))
    mask_2 = (key_2 <= qry_2) & (row0 + key_2 >= FIRST_KEY)
    pos_1 = jnp.concatenate([pos[:N_META], pos[:half]], axis=0)
    late = [slice(half, tq), slice(tq + half, 2 * tq)]
    lp = lam_ref[...]
    lam = (jnp.exp(jnp.sum(lp[0:1] * lp[1:2], axis=1, keepdims=True))
           - jnp.exp(jnp.sum(lp[2:3] * lp[3:4], axis=1, keepdims=True)) + lambda_init)
    row = row0 + lax.broadcasted_iota(jnp.int32, (tq, 1), 0)
    n_sum = ATTN_ACC_ROWS - DA_V_DIM

    def values_and_sums(v, p):
        return jnp.concatenate(
            [lax.dot_general(v, p.astype(BF16), TN_DIMS, preferred_element_type=F32),
             jnp.broadcast_to(jnp.sum(p, axis=0, keepdims=True), (n_sum, p.shape[1]))], axis=0)

    for hh, (head, m) in enumerate(zip(heads, ms)):
        cs, slope, q_t = head
        k_1 = jnp.concatenate([k_ref[pl.ds(FIRST_KEY, N_META), cs], k_ref[pl.ds(row0, half), cs]], axis=0)
        v_1 = jnp.concatenate([v_ref[pl.ds(FIRST_KEY, N_META), cs], v_ref[pl.ds(row0, half), cs]], axis=0)
        s_1 = jnp.dot(jnp.concatenate([k_1, pos_1], axis=1), q_t, preferred_element_type=F32)
        s_1 = s_1 + jnp.where(key_i[:, :1] < 0, slope * (FIRST_KEY - row0).astype(F32), 0.0)
        s_1 = jnp.where(mask_1, s_1, NEG)
        m = update(hh, m, s_1, jnp.max(s_1, axis=0, keepdims=True), 0.0,
                   functools.partial(values_and_sums, v_1))
        start_2 = pl.multiple_of(row0 + half, half)
        k_2 = jnp.concatenate([k_ref[pl.ds(start_2, half), cs], pos[half:]], axis=1)
        q_2 = jnp.concatenate([q_t[:, c] for c in late], axis=1)
        s_2 = jnp.where(mask_2, jnp.dot(k_2, q_2, preferred_element_type=F32), NEG)
        m_2 = jnp.concatenate([m[:, c] for c in late], axis=1)
        m_new = jnp.maximum(m_2, jnp.max(s_2, axis=0, keepdims=True))
        alpha = jnp.exp2(m_2 - m_new)
        pv_2 = values_and_sums(v_ref[pl.ds(start_2, half), cs], jnp.exp2(s_2 - m_new))
        for j, c in enumerate(late):
            jc = slice(j * half, (j + 1) * half)
            acc_ref[hh, :, c] = alpha[:, jc] * acc_ref[hh, :, c] + pv_2[:, jc]
        acc = acc_ref[hh]
        o_t = acc[:DA_V_DIM] * (1.0 / acc[DA_V_DIM:DA_V_DIM + 1])
        d = (o_t[:, :tq] - lam * o_t[:, tq:]).T
        d = jnp.where(row >= FIRST_KEY, d, 0.0)
        y = _rms_scale(d, sub_ref[...]) * (1.0 - lambda_init)
        o_ref[:, cs] = y.astype(o_ref.dtype)


def diff_attention(q_t, kv, lam_params, subln_w, lambda_init):
    B, Lp, _ = kv.shape
    hps = ATTN_HEADS_PER_STEP
    w = hps * DA_V_DIM
    n_col = DA_HEADS // hps
    n_tiles = Lp // ATTN_TQ
    slope = 2.0 ** (-8.0 * jnp.arange(1, DA_HEADS + 1, dtype=F32) / DA_HEADS) * LOG2E
    pieces, rest = [], slope
    for _ in range(SLOPE_PIECES):
        pieces.append(rest.astype(BF16).astype(F32))
        rest = rest - pieces[-1]
    slopes = jnp.stack(pieces + [slope], axis=1)
    kv_spec = lambda sec: pl.BlockSpec((None, Lp, w), lambda b, h, i: (b, 0, sec * n_col + h))
    return pl.pallas_call(
        functools.partial(_attn_body, n_tiles=n_tiles, lambda_init=lambda_init),
        grid=(B, n_col, n_tiles),
        in_specs=[pl.BlockSpec(memory_space=pltpu.SMEM),
                  pl.BlockSpec((w, ATTN_TQ), lambda b, h, i: (h, b * n_tiles + i)),
                  kv_spec(0),
                  kv_spec(1),
                  _resident((4, DA_HEAD_DIM)),
                  _resident((1, DA_V_DIM))],
        out_specs=pl.BlockSpec((None, ATTN_TQ, w), lambda b, h, i: (b, i, h)),
        out_shape=jax.ShapeDtypeStruct((B, Lp, DA_HEADS * DA_V_DIM), BF16),
        scratch_shapes=[pltpu.VMEM((hps, ATTN_TQ, 2 * ATTN_TQ), F32),
                        pltpu.VMEM((hps, ATTN_TQ, 2 * ATTN_TQ), F32),
                        pltpu.VMEM((hps, ATTN_ACC_ROWS, 2 * ATTN_TQ), F32)],
        compiler_params=pltpu.CompilerParams(
            dimension_semantics=("parallel", "parallel", "arbitrary"), vmem_limit_bytes=ATTN_VMEM_LIMIT),
        name="diff_attention",
    )(slopes, q_t, kv, kv, lam_params.astype(F32), subln_w.reshape(1, DA_V_DIM).astype(F32))


def _gla_tables():
    C = GLA_CHUNK
    i = np.arange(C)[:, None]
    t = np.arange(C)[None, :]
    blocks = [t <= i, t > i]
    level = np.full((C, C), GLA_LEVELS + 1, np.int32)
    level[np.arange(C), np.arange(C)] = GLA_LEVELS
    for l in range(GLA_LEVELS):
        s = C >> (l + 1)
        m = (i // (2 * s)) * 2 * s + s - 1
        upper = (i & s) != 0
        blocks.append(np.where(upper, (t > m) & (t <= i), (t > i) & (t <= m)))
        level[(i > t) & (((i ^ t) >> (GLA_LEVELS - 1 - l)) == 1)] = l
    sums = np.concatenate(blocks, axis=0).astype(np.float32)
    sums = np.concatenate([sums, sums], axis=1)
    return jnp.asarray(sums, BF16), jnp.asarray(level)


def _gla_body(level_ref, sums_ref, q_ref, k_ref, v_ref, g_ref, gl_ref, nw_ref, o_ref, st_ref, *, n_seqs):
    C = GLA_CHUNK

    @pl.when(pl.program_id(1) == 0)
    def _():
        st_ref[...] = jnp.zeros_like(st_ref)

    level = level_ref[...]
    at_level = [level == l for l in range(GLA_LEVELS + 1)]
    for cc in range(GLA_CHUNKS_PER_STEP):
        for seq in range(n_seqs):
            _gla_chunk(seq, slice(cc * C, (cc + 1) * C), at_level, sums_ref, q_ref, k_ref, v_ref, g_ref,
                       gl_ref, nw_ref, o_ref, st_ref)


def _gla_chunk(seq, rows, at_level, sums_ref, q_ref, k_ref, v_ref, g_ref, gl_ref, nw_ref, o_ref, st_ref):
    C = GLA_CHUNK
    gl = gl_ref[seq, rows, :]
    g_hi = gl.astype(BF16)
    g_lo = (gl - g_hi.astype(F32)).astype(BF16)
    w_all = jnp.dot(sums_ref[...], jnp.concatenate([g_hi, g_lo], axis=0),
                    preferred_element_type=F32)

    key_cols = [slice(h * GLA_HK, (h + 1) * GLA_HK) for h in range(GLA_HEADS)]
    qs = [q_ref[seq, rows, ks].astype(F32) for ks in key_cols]
    ks_ = [k_ref[seq, rows, ks].astype(F32) for ks in key_cols]
    ws = [w_all[:, ks] for ks in key_cols]
    scores = [jnp.where(at_level[GLA_LEVELS], jnp.sum(q * k, axis=1, keepdims=True), 0.0)
              for q, k in zip(qs, ks_)]
    zero = jnp.zeros((C, GLA_HK), BF16)
    for h0 in range(0, GLA_HEADS, 2):
        for l in range(GLA_LEVELS):
            e = [jnp.exp2(ws[h][(2 + l) * C:(3 + l) * C]) for h in (h0, h0 + 1)]
            qe2 = jnp.concatenate([(qs[h0] * e[0]).astype(BF16), (qs[h0 + 1] * e[1]).astype(BF16)], axis=1)
            ke2 = jnp.concatenate(
                [jnp.concatenate([(ks_[h0] * e[0]).astype(BF16), zero], axis=1),
                 jnp.concatenate([zero, (ks_[h0 + 1] * e[1]).astype(BF16)], axis=1)], axis=0)
            a_l = lax.dot_general(qe2, ke2, NT_DIMS, preferred_element_type=F32)
            scores[h0] = jnp.where(at_level[l], a_l[:, :C], scores[h0])
            scores[h0 + 1] = jnp.where(at_level[l], a_l[:, C:], scores[h0 + 1])

    for h in range(GLA_HEADS):
        vs = slice(h * GLA_HV, (h + 1) * GLA_HV)
        q, k, w, a = qs[h], ks_[h], ws[h], scores[h]
        v = v_ref[seq, rows, vs]
        cum = w[0:C]
        state_t = st_ref[seq, h]
        qe = (q * jnp.exp2(cum)).astype(BF16)
        o = (jnp.dot(a.astype(BF16), v, preferred_element_type=F32)
             + lax.dot_general(qe, state_t.astype(BF16), NT_DIMS, preferred_element_type=F32))
        kd = (k * jnp.exp2(w[C:2 * C])).astype(BF16)
        st_ref[seq, h] = (state_t * jnp.exp2(cum[C - 1:C])
                     + lax.dot_general(v, kd, TN_DIMS, preferred_element_type=F32))
        g = g_ref[seq, rows, vs].astype(F32)
        y = _rms_scale(o, nw_ref[...]) * (g / (1.0 + jnp.exp(-g)))
        o_ref[seq, rows, vs] = y.astype(o_ref.dtype)


def gla_mixer(qkvg, glog, norm_w):
    B, Lp, _ = qkvg.shape
    C = GLA_CHUNK
    sums, level = _gla_tables()
    kw, vw = GLA_KEY_DIM, GLA_VAL_DIM
    R = GLA_CHUNKS_PER_STEP * C
    S = GLA_SEQS_PER_STEP if B % GLA_SEQS_PER_STEP == 0 else 1
    return pl.pallas_call(
        functools.partial(_gla_body, n_seqs=S),
        grid=(B // S, Lp // R),
        in_specs=[_resident((C, C)),
                  _resident(((2 + GLA_LEVELS) * C, 2 * C)),
                  pl.BlockSpec((S, R, kw), lambda b, c: (b, c, 0)),
                  pl.BlockSpec((S, R, kw), lambda b, c: (b, c, 1)),
                  pl.BlockSpec((S, R, vw), lambda b, c: (b, c, 1)),
                  pl.BlockSpec((S, R, vw), lambda b, c: (b, c, 2)),
                  pl.BlockSpec((S, R, kw), lambda b, c: (b, c, 0)),
                  _resident((1, GLA_HV))],
        out_specs=pl.BlockSpec((S, R, vw), lambda b, c: (b, c, 0)),
        out_shape=jax.ShapeDtypeStruct((B, Lp, vw), BF16),
        scratch_shapes=[pltpu.VMEM((S, GLA_HEADS, GLA_HV, GLA_HK), F32)],
        compiler_params=pltpu.CompilerParams(
            dimension_semantics=("parallel", "arbitrary"), vmem_limit_bytes=VMEM_LIMIT),
        name="gla_mixer",
    )(level, sums, qkvg, qkvg, qkvg, qkvg, glog, norm_w.reshape(1, GLA_HV).astype(F32))


def _out_mlp_body(h_ref, o_ref, wo_ref, nw_ref, wu_ref, wd_ref, *rest, f_chunk):
    out_ref = rest[-1]
    h1 = h_ref[...] + jnp.dot(o_ref[...], wo_ref[...], preferred_element_type=F32)
    hn = _rms_scale(h1, nw_ref[...]).astype(BF16)
    out_ref[...] = h1
    for c in range(0, wu_ref.shape[1], f_chunk):
        u = jnp.dot(hn, wu_ref[:, c:c + f_chunk], preferred_element_type=F32)
        u = jnp.square(jnp.maximum(u, 0.0)).astype(BF16)
        out_ref[...] += jnp.dot(u, wd_ref[c:c + f_chunk, :], preferred_element_type=F32)
    if len(rest) == 2:
        out_ref[...] = _rms_scale(out_ref[...], rest[0][...])


def out_mlp(h, o, w_out, norm_w, w_up, w_down):
    T, D = h.shape
    F = w_up.shape[1]
    tm = _row_tile(T)
    return pl.pallas_call(
        functools.partial(_out_mlp_body, f_chunk=MLP_F_CHUNK),
        grid=(T // tm,),
        in_specs=[pl.BlockSpec((tm, D), lambda i: (i, 0)),
                  pl.BlockSpec((tm, D), lambda i: (i, 0)),
                  _resident((D, D)),
                  _resident((1, D)),
                  _resident((D, F)),
                  _resident((F, D))],
        out_specs=pl.BlockSpec((tm, D), lambda i: (i, 0)),
        out_shape=jax.ShapeDtypeStruct((T, D), F32),
        compiler_params=pltpu.CompilerParams(
            dimension_semantics=("parallel",), vmem_limit_bytes=VMEM_LIMIT),
        name="out_mlp",
    )(h, o, w_out, norm_w.reshape(1, D), w_up, w_down)


def out_mlp_final(h, o, w_out, norm_w, w_up, w_down, final_w):
    B, Lp, D = h.shape
    F = w_up.shape[1]
    t = PAD
    row_spec = pl.BlockSpec((None, t, D), lambda b, r: (b, r, 0))
    return pl.pallas_call(
        functools.partial(_out_mlp_body, f_chunk=MLP_F_CHUNK),
        grid=(B, Lp // t),
        in_specs=[row_spec, row_spec,
                  _resident((D, D)),
                  _resident((1, D)),
                  _resident((D, F)),
                  _resident((F, D)),
                  _resident((1, D))],
        out_specs=pl.BlockSpec((None, t, D), lambda b, r: (b, jnp.maximum(r - 1, 0), 0)),
        out_shape=jax.ShapeDtypeStruct((B, Lp - PAD, D), F32),
        compiler_params=pltpu.CompilerParams(
            dimension_semantics=("parallel", "arbitrary"), vmem_limit_bytes=VMEM_LIMIT),
        name="out_mlp_final",
    )(h, o, w_out, norm_w.reshape(1, D), w_up, w_down, final_w.reshape(1, D))


def _lambda_init_for(layer_idx):
    return 0.8 - 0.6 * math.exp(-0.3 * layer_idx)


def kernel(x, meta_tokens, mix_norm_w, attn_w_in, attn_lambda, attn_subln_w, attn_w_out, gla_w_in, gla_w_gate_up, gla_gate_bias, gla_norm_w, gla_w_out, mlp_norm_w, mlp_w_up, mlp_w_down, final_norm_w):
    B, seq, D = x.shape
    Lp = PAD + seq
    T = B * Lp
    lead = jnp.concatenate([jnp.zeros((FIRST_KEY, D), x.dtype), meta_tokens.astype(x.dtype)], axis=0)

    for i in range(DEPTH):
        j = i // 2
        if i % 2 == 0:
            col_scale = jnp.where(jnp.arange(3 * D) < D, ATTN_Q_SCALE, 1.0).astype(F32)
            w_in = (attn_w_in[j] * col_scale).astype(BF16)
            if i == 0:
                h, q_t, kv = norm_proj_first(x, lead, mix_norm_w[i], w_in)
            else:
                q_t, kv = norm_proj(h, mix_norm_w[i], w_in)
            o = diff_attention(q_t, kv.reshape(B, Lp, 2 * D), attn_lambda[j], attn_subln_w[j],
                               _lambda_init_for(i))
            w_out = attn_w_out[j]
        else:
            n_main = 2 * GLA_KEY_DIM + 2 * GLA_VAL_DIM
            w_in = gla_w_in[j]
            w_gz = jnp.pad(w_in[:, n_main:], ((0, 0), (0, LANE - GLA_GATE_RANK))).astype(BF16)
            w_gu = jnp.pad(gla_w_gate_up[j], ((0, LANE - GLA_GATE_RANK), (0, 0))).astype(BF16)
            q_scale = jnp.where(jnp.arange(n_main) < GLA_KEY_DIM, GLA_HK ** -0.5, 1.0).astype(F32)
            qkvg, glog = gla_proj(h, mix_norm_w[i], (w_in[:, :n_main] * q_scale).astype(BF16), w_gz, w_gu,
                                  gla_gate_bias[j])
            o = gla_mixer(qkvg.reshape(B, Lp, n_main), glog.reshape(B, Lp, GLA_KEY_DIM), gla_norm_w[j])
            w_out = gla_w_out[j]
        mlp_args = (w_out.astype(BF16), mlp_norm_w[i], mlp_w_up[i].astype(BF16), mlp_w_down[i].astype(BF16))
        if i < DEPTH - 1:
            h = out_mlp(h, o.reshape(T, D), *mlp_args)
    return out_mlp_final(h.reshape(B, Lp, D), o.reshape(B, Lp, D), *mlp_args, final_norm_w)
```

```python
import functools
import math

import numpy as np
import jax
import jax.numpy as jnp
from jax import lax
from jax.experimental import pallas as pl
from jax.experimental.pallas import tpu as pltpu

F32 = jnp.float32
BF16 = jnp.bfloat16

DEPTH = 4
N_META = 16
ATTN_TQ = 256
PAD = ATTN_TQ
FIRST_KEY = PAD - N_META
EPS = 1e-6
NEG = -1e30
LOG2E = math.log2(math.e)

DA_HEADS = 8
DA_HEAD_DIM = 64
DA_V_DIM = 128
GLA_HEADS = 4
GLA_HK = 128
GLA_HV = 256
GLA_KEY_DIM = 512
GLA_VAL_DIM = 1024
GLA_GATE_RANK = 16
GLA_GATE_NORM = 16.0
GLA_CHUNK = 128
GLA_LEVELS = 7
GLA_CHUNKS_PER_STEP = 2
GLA_SEQS_PER_STEP = 2
PROJ_N_CHUNK = 512
MLP_F_CHUNK = 1024

LANE = 128
VMEM_LIMIT = 50 * 1024 * 1024
ATTN_VMEM_LIMIT = 57 * 1024 * 1024

NT_DIMS = (((1,), (1,)), ((), ()))
TN_DIMS = (((0,), (0,)), ((), ()))


def _row_tile(n_rows):
    for t in (512, 384, 256, 128):
        if n_rows % t == 0:
            return t
    raise ValueError(f"row count {n_rows} is not a multiple of 128")


def _rms_scale(x, w):
    return x * lax.rsqrt(jnp.mean(x * x, axis=-1, keepdims=True) + EPS) * w


def _resident(shape):
    return pl.BlockSpec(shape, lambda *_: (0,) * len(shape), pipeline_mode=pl.Buffered(1))


def _norm_proj_first_body(x_ref, lead_ref, nw_ref, w_ref, h_ref, qt_ref, kv_ref, *, n_chunk):
    @pl.when(pl.program_id(1) == 0)
    def _():
        h_ref[...] = lead_ref[...]

    @pl.when(pl.program_id(1) > 0)
    def _():
        h_ref[...] = x_ref[...]

    _norm_proj_body(h_ref, nw_ref, w_ref, qt_ref, kv_ref, n_chunk=n_chunk)


def norm_proj_first(x, lead, norm_w, w):
    B, seq, D = x.shape
    N = w.shape[1]
    t = PAD
    n = (PAD + seq) // t
    T = B * n * t
    return pl.pallas_call(
        functools.partial(_norm_proj_first_body, n_chunk=PROJ_N_CHUNK),
        grid=(B, n),
        in_specs=[pl.BlockSpec((None, t, D), lambda b, r: (b, jnp.maximum(r - 1, 0), 0)),
                  _resident((t, D)),
                  _resident((1, D)),
                  _resident((D, N))],
        out_specs=[pl.BlockSpec((t, D), lambda b, r: (b * n + r, 0)),
                   pl.BlockSpec((D, t), lambda b, r: (0, b * n + r)),
                   pl.BlockSpec((t, N - D), lambda b, r: (b * n + r, 0))],
        out_shape=[jax.ShapeDtypeStruct((T, D), F32),
                   jax.ShapeDtypeStruct((D, T), BF16),
                   jax.ShapeDtypeStruct((T, N - D), BF16)],
        compiler_params=pltpu.CompilerParams(
            dimension_semantics=("parallel", "parallel"), vmem_limit_bytes=VMEM_LIMIT),
        name="norm_proj_first",
    )(x, lead, norm_w.reshape(1, D), w)


def _norm_proj_body(h_ref, nw_ref, w_ref, qt_ref, kv_ref, *, n_chunk):
    hn = _rms_scale(h_ref[...], nw_ref[...]).astype(BF16)
    n_q = qt_ref.shape[0]
    for c in range(0, n_q, n_chunk):
        q = jnp.dot(hn, w_ref[:, c:c + n_chunk], preferred_element_type=F32)
        qt_ref[c:c + n_chunk, :] = q.T.astype(qt_ref.dtype)
    for c in range(n_q, w_ref.shape[1], n_chunk):
        kv_ref[:, c - n_q:c - n_q + n_chunk] = jnp.dot(
            hn, w_ref[:, c:c + n_chunk], preferred_element_type=F32).astype(kv_ref.dtype)


def norm_proj(h, norm_w, w):
    T, D = h.shape
    N = w.shape[1]
    tm = _row_tile(T)
    return pl.pallas_call(
        functools.partial(_norm_proj_body, n_chunk=PROJ_N_CHUNK),
        grid=(T // tm,),
        in_specs=[pl.BlockSpec((tm, D), lambda i: (i, 0)),
                  _resident((1, D)),
                  _resident((D, N))],
        out_specs=[pl.BlockSpec((D, tm), lambda i: (0, i)),
                   pl.BlockSpec((tm, N - D), lambda i: (i, 0))],
        out_shape=[jax.ShapeDtypeStruct((D, T), BF16),
                   jax.ShapeDtypeStruct((T, N - D), BF16)],
        compiler_params=pltpu.CompilerParams(
            dimension_semantics=("parallel",), vmem_limit_bytes=VMEM_LIMIT),
        name="norm_proj",
    )(h, norm_w.reshape(1, D), w)


def _gla_proj_body(h_ref, nw_ref, w_ref, wgz_ref, wgu_ref, gb_ref, o_ref, gl_ref, *, n_chunk):
    hn = _rms_scale(h_ref[...], nw_ref[...]).astype(BF16)
    for c in range(0, w_ref.shape[1], n_chunk):
        o_ref[:, c:c + n_chunk] = jnp.dot(
            hn, w_ref[:, c:c + n_chunk], preferred_element_type=F32).astype(o_ref.dtype)
    gz = jnp.dot(hn, wgz_ref[...], preferred_element_type=F32).astype(BF16)
    z = jnp.dot(gz, wgu_ref[...], preferred_element_type=F32) + gb_ref[...]
    gl_ref[...] = (jnp.minimum(z, 0.0) - jnp.log1p(jnp.exp(-jnp.abs(z)))) * (LOG2E / GLA_GATE_NORM)


def gla_proj(h, norm_w, w_main, w_gz, w_gate_up, gate_bias):
    T, D = h.shape
    N = w_main.shape[1]
    tm = _row_tile(T)
    return pl.pallas_call(
        functools.partial(_gla_proj_body, n_chunk=PROJ_N_CHUNK),
        grid=(T // tm,),
        in_specs=[pl.BlockSpec((tm, D), lambda i: (i, 0)),
                  _resident((1, D)),
                  _resident((D, N)),
                  _resident((D, LANE)),
                  _resident((LANE, GLA_KEY_DIM)),
                  _resident((1, GLA_KEY_DIM))],
        out_specs=[pl.BlockSpec((tm, N), lambda i: (i, 0)),
                   pl.BlockSpec((tm, GLA_KEY_DIM), lambda i: (i, 0))],
        out_shape=[jax.ShapeDtypeStruct((T, N), BF16),
                   jax.ShapeDtypeStruct((T, GLA_KEY_DIM), F32)],
        compiler_params=pltpu.CompilerParams(
            dimension_semantics=("parallel",), vmem_limit_bytes=VMEM_LIMIT),
        name="gla_proj",
    )(h, norm_w.reshape(1, D), w_main, w_gz, w_gate_up, gate_bias.reshape(1, GLA_KEY_DIM))


ATTN_HEADS_PER_STEP = 8
ATTN_ACC_ROWS = DA_V_DIM + 16
SLOPE_PIECES = 3
ATTN_Q_SCALE = DA_HEAD_DIM ** -0.5 * LOG2E


def _attn_body(slopes_ref, q_ref, k_ref, v_ref, lam_ref, sub_ref, o_ref, sa_ref, sb_ref, acc_ref, *,
               n_tiles, lambda_init):
    tq = ATTN_TQ
    hp = pl.program_id(1)
    qi = pl.program_id(2)
    row0 = pl.multiple_of(qi * tq, tq)
    lane = lax.broadcasted_iota(jnp.int32, (tq, LANE), 1)
    sub = lax.broadcasted_iota(jnp.int32, (tq, LANE), 0)
    pos = jnp.where(lane < SLOPE_PIECES, sub, 0).astype(BF16)
    ones_rows = jnp.ones((ATTN_ACC_ROWS - DA_V_DIM, tq), BF16)

    dim_i = lax.broadcasted_iota(jnp.int32, (DA_V_DIM, tq), 0)
    feat_i = lax.broadcasted_iota(jnp.int32, (LANE, 2 * tq), 0)
    heads = []
    for hh in range(ATTN_HEADS_PER_STEP):
        cs = slice(hh * DA_V_DIM, (hh + 1) * DA_V_DIM)
        h = ATTN_HEADS_PER_STEP * hp + hh
        q = q_ref[cs, :].astype(F32)
        feat = jnp.zeros((LANE, 2 * tq), F32)
        for j in range(SLOPE_PIECES):
            feat = jnp.where(feat_i == j, slopes_ref[h, j], feat)
        q_t = jnp.concatenate(
            [jnp.concatenate([jnp.where(dim_i < DA_HEAD_DIM, q, 0.0),
                              jnp.where(dim_i >= DA_HEAD_DIM, q, 0.0)], axis=1), feat], axis=0)
        heads.append((cs, slopes_ref[h, SLOPE_PIECES], q_t.astype(BF16)))

    def scores(head, start):
        k_aug = jnp.concatenate([k_ref[pl.ds(start, tq), head[0]], pos], axis=1)
        s = jnp.dot(k_aug, head[2], preferred_element_type=F32)
        return s, jnp.max(s, axis=0, keepdims=True)

    def update(hh, m, s, s_max, c, pv_and_sum):
        m_new = jnp.maximum(m, s_max + c)
        alpha = jnp.exp2(m - m_new)
        p = jnp.exp2(s - (m_new - c))
        acc_ref[hh] = alpha * acc_ref[hh] + pv_and_sum(p)
        return m_new

    def tile_start(t):
        return pl.multiple_of(t * tq, tq)

    def full_tile(hh, m, s, s_max, start):
        c = heads[hh][1] * (start - row0).astype(F32)
        v_t = jnp.concatenate([v_ref[pl.ds(start, tq), heads[hh][0]].T, ones_rows], axis=0)
        return update(hh, m, s, s_max, c,
                      lambda p: jnp.dot(v_t, p.astype(BF16), preferred_element_type=F32))

    n_full = qi - 1
    trips = jnp.maximum(n_full - 1, 0) // 2
    carries = []
    for hh, head in enumerate(heads):
        acc_ref[hh] = jnp.zeros((ATTN_ACC_ROWS, 2 * tq), F32)
        s, s_max = scores(head, tile_start(1))
        sa_ref[hh] = s
        carries.append((jnp.full((1, 2 * tq), NEG, F32), s_max))

    def issue(buf_ref, t):
        maxes = []
        for hh, head in enumerate(heads):
            s, s_max = scores(head, tile_start(t))
            buf_ref[hh] = s
            maxes.append(s_max)
        return maxes

    def consume(buf_ref, t, ms, maxes, off=None):
        out = []
        for hh, (m, s_max) in enumerate(zip(ms, maxes)):
            s = buf_ref[hh]
            if off is not None:
                s, s_max = s + off, s_max + off
            out.append(full_tile(hh, m, s, s_max, tile_start(t)))
        return out

    def body(i, carries):
        t = 1 + 2 * i
        max_b = issue(sb_ref, t + 1)
        ms = consume(sa_ref, t, [c[0] for c in carries], [c[1] for c in carries])
        max_a = issue(sa_ref, t + 2)
        ms = consume(sb_ref, t + 1, ms, max_b)
        return tuple(zip(ms, max_a))

    carries = lax.fori_loop(0, trips, body, tuple(carries))

    t_a = 1 + 2 * trips
    remaining = n_full - 2 * trips
    ms = consume(sa_ref, t_a, [c[0] for c in carries], [c[1] for c in carries],
                 jnp.where(remaining >= 1, 0.0, NEG))

    def second_tile(ms):
        return tuple(consume(sb_ref, t_a + 1, list(ms), issue(sb_ref, t_a + 1)))

    ms = lax.cond(remaining >= 2, second_tile, lambda ms: ms, tuple(ms))

    half = tq // 2
    key_i = lax.broadcasted_iota(jnp.int32, (N_META + half, 2 * tq), 0) - N_META
    qry_i = lax.broadcasted_iota(jnp.int32, (N_META + half, 2 * tq), 1) & (tq - 1)
    meta_base = jnp.where(qi >= 1, PAD, -PAD)
    key_row = jnp.where(key_i < 0, meta_base + key_i, row0 + key_i)
    mask_1 = (key_row <= row0 + qry_i) & (key_row >= FIRST_KEY)
    key_2 = half + lax.broadcasted_iota(jnp.int32, (half, tq), 0)
    qry_2 = half + (lax.broadcasted_iota(jnp.int32, (half, tq), 1) & (half - 1))
    mask_2 = (key_2 <= qry_2) & (row0 + key_2 >= FIRST_KEY)
    pos_1 = jnp.concatenate([pos[:N_META], pos[:half]], axis=0)
    late = [slice(half, tq), slice(tq + half, 2 * tq)]
    lp = lam_ref[...]
    lam = (jnp.exp(jnp.sum(lp[0:1] * lp[1:2], axis=1, keepdims=True))
           - jnp.exp(jnp.sum(lp[2:3] * lp[3:4], axis=1, keepdims=True)) + lambda_init)
    row = row0 + lax.broadcasted_iota(jnp.int32, (tq, 1), 0)
    n_sum = ATTN_ACC_ROWS - DA_V_DIM

    def values_and_sums(v, p):
        return jnp.concatenate(
            [lax.dot_general(v, p.astype(BF16), TN_DIMS, preferred_element_type=F32),
             jnp.broadcast_to(jnp.sum(p, axis=0, keepdims=True), (n_sum, p.shape[1]))], axis=0)

    for hh, (head, m) in enumerate(zip(heads, ms)):
        cs, slope, q_t = head
        k_1 = jnp.concatenate([k_ref[pl.ds(FIRST_KEY, N_META), cs], k_ref[pl.ds(row0, half), cs]], axis=0)
        v_1 = jnp.concatenate([v_ref[pl.ds(FIRST_KEY, N_META), cs], v_ref[pl.ds(row0, half), cs]], axis=0)
        s_1 = jnp.dot(jnp.concatenate([k_1, pos_1], axis=1), q_t, preferred_element_type=F32)
        s_1 = s_1 + jnp.where(key_i[:, :1] < 0, slope * (FIRST_KEY - row0).astype(F32), 0.0)
        s_1 = jnp.where(mask_1, s_1, NEG)
        m = update(hh, m, s_1, jnp.max(s_1, axis=0, keepdims=True), 0.0,
                   functools.partial(values_and_sums, v_1))
        start_2 = pl.multiple_of(row0 + half, half)
        k_2 = jnp.concatenate([k_ref[pl.ds(start_2, half), cs], pos[half:]], axis=1)
        q_2 = jnp.concatenate([q_t[:, c] for c in late], axis=1)
        s_2 = jnp.where(mask_2, jnp.dot(k_2, q_2, preferred_element_type=F32), NEG)
        m_2 = jnp.concatenate([m[:, c] for c in late], axis=1)
        m_new = jnp.maximum(m_2, jnp.max(s_2, axis=0, keepdims=True))
        alpha = jnp.exp2(m_2 - m_new)
        pv_2 = values_and_sums(v_ref[pl.ds(start_2, half), cs], jnp.exp2(s_2 - m_new))
        for j, c in enumerate(late):
            jc = slice(j * half, (j + 1) * half)
            acc_ref[hh, :, c] = alpha[:, jc] * acc_ref[hh, :, c] + pv_2[:, jc]
        acc = acc_ref[hh]
        o_t = acc[:DA_V_DIM] * (1.0 / acc[DA_V_DIM:DA_V_DIM + 1])
        d = (o_t[:, :tq] - lam * o_t[:, tq:]).T
        d = jnp.where(row >= FIRST_KEY, d, 0.0)
        y = _rms_scale(d, sub_ref[...]) * (1.0 - lambda_init)
        o_ref[:, cs] = y.astype(o_ref.dtype)


def diff_attention(q_t, kv, lam_params, subln_w, lambda_init):
    B, Lp, _ = kv.shape
    hps = ATTN_HEADS_PER_STEP
    w = hps * DA_V_DIM
    n_col = DA_HEADS // hps
    n_tiles = Lp // ATTN_TQ
    slope = 2.0 ** (-8.0 * jnp.arange(1, DA_HEADS + 1, dtype=F32) / DA_HEADS) * LOG2E
    pieces, rest = [], slope
    for _ in range(SLOPE_PIECES):
        pieces.append(rest.astype(BF16).astype(F32))
        rest = rest - pieces[-1]
    slopes = jnp.stack(pieces + [slope], axis=1)
    kv_spec = lambda sec: pl.BlockSpec((None, Lp, w), lambda b, h, i: (b, 0, sec * n_col + h))
    return pl.pallas_call(
        functools.partial(_attn_body, n_tiles=n_tiles, lambda_init=lambda_init),
        grid=(B, n_col, n_tiles),
        in_specs=[pl.BlockSpec(memory_space=pltpu.SMEM),
                  pl.BlockSpec((w, ATTN_TQ), lambda b, h, i: (h, b * n_tiles + i)),
                  kv_spec(0),
                  kv_spec(1),
                  _resident((4, DA_HEAD_DIM)),
                  _resident((1, DA_V_DIM))],
        out_specs=pl.BlockSpec((None, ATTN_TQ, w), lambda b, h, i: (b, i, h)),
        out_shape=jax.ShapeDtypeStruct((B, Lp, DA_HEADS * DA_V_DIM), BF16),
        scratch_shapes=[pltpu.VMEM((hps, ATTN_TQ, 2 * ATTN_TQ), F32),
                        pltpu.VMEM((hps, ATTN_TQ, 2 * ATTN_TQ), F32),
                        pltpu.VMEM((hps, ATTN_ACC_ROWS, 2 * ATTN_TQ), F32)],
        compiler_params=pltpu.CompilerParams(
            dimension_semantics=("parallel", "parallel", "arbitrary"), vmem_limit_bytes=ATTN_VMEM_LIMIT),
        name="diff_attention",
    )(slopes, q_t, kv, kv, lam_params.astype(F32), subln_w.reshape(1, DA_V_DIM).astype(F32))


def _gla_tables():
    C = GLA_CHUNK
    i = np.arange(C)[:, None]
    t = np.arange(C)[None, :]
    blocks = [t <= i, t > i]
    level = np.full((C, C), GLA_LEVELS + 1, np.int32)
    level[np.arange(C), np.arange(C)] = GLA_LEVELS
    for l in range(GLA_LEVELS):
        s = C >> (l + 1)
        m = (i // (2 * s)) * 2 * s + s - 1
        upper = (i & s) != 0
        blocks.append(np.where(upper, (t > m) & (t <= i), (t > i) & (t <= m)))
        level[(i > t) & (((i ^ t) >> (GLA_LEVELS - 1 - l)) == 1)] = l
    sums = np.concatenate(blocks, axis=0).astype(np.float32)
    sums = np.concatenate([sums, sums], axis=1)
    return jnp.asarray(sums, BF16), jnp.asarray(level)


def _gla_body(level_ref, sums_ref, q_ref, k_ref, v_ref, g_ref, gl_ref, nw_ref, o_ref, st_ref, *, n_seqs):
    C = GLA_CHUNK

    @pl.when(pl.program_id(1) == 0)
    def _():
        st_ref[...] = jnp.zeros_like(st_ref)

    level = level_ref[...]
    at_level = [level == l for l in range(GLA_LEVELS + 1)]
    for cc in range(GLA_CHUNKS_PER_STEP):
        for seq in range(n_seqs):
            _gla_chunk(seq, slice(cc * C, (cc + 1) * C), at_level, sums_ref, q_ref, k_ref, v_ref, g_ref,
                       gl_ref, nw_ref, o_ref, st_ref)


def _gla_chunk(seq, rows, at_level, sums_ref, q_ref, k_ref, v_ref, g_ref, gl_ref, nw_ref, o_ref, st_ref):
    C = GLA_CHUNK
    gl = gl_ref[seq, rows, :]
    g_hi = gl.astype(BF16)
    g_lo = (gl - g_hi.astype(F32)).astype(BF16)
    w_all = jnp.dot(sums_ref[...], jnp.concatenate([g_hi, g_lo], axis=0),
                    preferred_element_type=F32)

    key_cols = [slice(h * GLA_HK, (h + 1) * GLA_HK) for h in range(GLA_HEADS)]
    qs = [q_ref[seq, rows, ks].astype(F32) for ks in key_cols]
    ks_ = [k_ref[seq, rows, ks].astype(F32) for ks in key_cols]
    ws = [w_all[:, ks] for ks in key_cols]
    scores = [jnp.where(at_level[GLA_LEVELS], jnp.sum(q * k, axis=1, keepdims=True), 0.0)
              for q, k in zip(qs, ks_)]
    zero = jnp.zeros((C, GLA_HK), BF16)
    for h0 in range(0, GLA_HEADS, 2):
        for l in range(GLA_LEVELS):
            e = [jnp.exp2(ws[h][(2 + l) * C:(3 + l) * C]) for h in (h0, h0 + 1)]
            qe2 = jnp.concatenate([(qs[h0] * e[0]).astype(BF16), (qs[h0 + 1] * e[1]).astype(BF16)], axis=1)
            ke2 = jnp.concatenate(
                [jnp.concatenate([(ks_[h0] * e[0]).astype(BF16), zero], axis=1),
                 jnp.concatenate([zero, (ks_[h0 + 1] * e[1]).astype(BF16)], axis=1)], axis=0)
            a_l = lax.dot_general(qe2, ke2, NT_DIMS, preferred_element_type=F32)
            scores[h0] = jnp.where(at_level[l], a_l[:, :C], scores[h0])
            scores[h0 + 1] = jnp.where(at_level[l], a_l[:, C:], scores[h0 + 1])

    for h in range(GLA_HEADS):
        vs = slice(h * GLA_HV, (h + 1) * GLA_HV)
        q, k, w, a = qs[h], ks_[h], ws[h], scores[h]
        v = v_ref[seq, rows, vs]
        cum = w[0:C]
        state_t = st_ref[seq, h]
        qe = (q * jnp.exp2(cum)).astype(BF16)
        o = (jnp.dot(a.astype(BF16), v, preferred_element_type=F32)
             + lax.dot_general(qe, state_t.astype(BF16), NT_DIMS, preferred_element_type=F32))
        kd = (k * jnp.exp2(w[C:2 * C])).astype(BF16)
        st_ref[seq, h] = (state_t * jnp.exp2(cum[C - 1:C])
                     + lax.dot_general(v, kd, TN_DIMS, preferred_element_type=F32))
        g = g_ref[seq, rows, vs].astype(F32)
        y = _rms_scale(o, nw_ref[...]) * (g / (1.0 + jnp.exp(-g)))
        o_ref[seq, rows, vs] = y.astype(o_ref.dtype)


def gla_mixer(qkvg, glog, norm_w):
    B, Lp, _ = qkvg.shape
    C = GLA_CHUNK
    sums, level = _gla_tables()
    kw, vw = GLA_KEY_DIM, GLA_VAL_DIM
    R = GLA_CHUNKS_PER_STEP * C
    S = GLA_SEQS_PER_STEP if B % GLA_SEQS_PER_STEP == 0 else 1
    return pl.pallas_call(
        functools.partial(_gla_body, n_seqs=S),
        grid=(B // S, Lp // R),
        in_specs=[_resident((C, C)),
                  _resident(((2 + GLA_LEVELS) * C, 2 * C)),
                  pl.BlockSpec((S, R, kw), lambda b, c: (b, c, 0)),
                  pl.BlockSpec((S, R, kw), lambda b, c: (b, c, 1)),
                  pl.BlockSpec((S, R, vw), lambda b, c: (b, c, 1)),
                  pl.BlockSpec((S, R, vw), lambda b, c: (b, c, 2)),
                  pl.BlockSpec((S, R, kw), lambda b, c: (b, c, 0)),
                  _resident((1, GLA_HV))],
        out_specs=pl.BlockSpec((S, R, vw), lambda b, c: (b, c, 0)),
        out_shape=jax.ShapeDtypeStruct((B, Lp, vw), BF16),
        scratch_shapes=[pltpu.VMEM((S, GLA_HEADS, GLA_HV, GLA_HK), F32)],
        compiler_params=pltpu.CompilerParams(
            dimension_semantics=("parallel", "arbitrary"), vmem_limit_bytes=VMEM_LIMIT),
        name="gla_mixer",
    )(level, sums, qkvg, qkvg, qkvg, qkvg, glog, norm_w.reshape(1, GLA_HV).astype(F32))


def _mlp_rows(h, o, wo_ref, nw_ref, wu_ref, wd_ref, out_ref, f_chunk):
    h1 = h + jnp.dot(o, wo_ref[...], preferred_element_type=F32)
    hn = _rms_scale(h1, nw_ref[...]).astype(BF16)
    out_ref[...] = h1
    for c in range(0, wu_ref.shape[1], f_chunk):
        u = jnp.dot(hn, wu_ref[:, c:c + f_chunk], preferred_element_type=F32)
        u = jnp.square(jnp.maximum(u, 0.0)).astype(BF16)
        out_ref[...] += jnp.dot(u, wd_ref[c:c + f_chunk, :], preferred_element_type=F32)


def _out_mlp_body(h_ref, o_ref, wo_ref, nw_ref, wu_ref, wd_ref, out_ref, *, f_chunk):
    _mlp_rows(h_ref[...], o_ref[...], wo_ref, nw_ref, wu_ref, wd_ref, out_ref, f_chunk)


def _out_mlp_final_body(ha_ref, hb_ref, oa_ref, ob_ref, wo_ref, nw_ref, wu_ref, wd_ref, fw_ref, out_ref, *,
                        f_chunk):
    h = jnp.concatenate([ha_ref[...], hb_ref[...]], axis=0)
    o = jnp.concatenate([oa_ref[...], ob_ref[...]], axis=0)
    _mlp_rows(h, o, wo_ref, nw_ref, wu_ref, wd_ref, out_ref, f_chunk)
    out_ref[...] = _rms_scale(out_ref[...], fw_ref[...])


def out_mlp(h, o, w_out, norm_w, w_up, w_down):
    T, D = h.shape
    F = w_up.shape[1]
    tm = _row_tile(T)
    return pl.pallas_call(
        functools.partial(_out_mlp_body, f_chunk=MLP_F_CHUNK),
        grid=(T // tm,),
        in_specs=[pl.BlockSpec((tm, D), lambda i: (i, 0)),
                  pl.BlockSpec((tm, D), lambda i: (i, 0)),
                  _resident((D, D)),
                  _resident((1, D)),
                  _resident((D, F)),
                  _resident((F, D))],
        out_specs=pl.BlockSpec((tm, D), lambda i: (i, 0)),
        out_shape=jax.ShapeDtypeStruct((T, D), F32),
        compiler_params=pltpu.CompilerParams(
            dimension_semantics=("parallel",), vmem_limit_bytes=VMEM_LIMIT),
        name="out_mlp",
    )(h, o, w_out, norm_w.reshape(1, D), w_up, w_down)


def out_mlp_final(h, o, w_out, norm_w, w_up, w_down, final_w):
    B, Lp, D = h.shape
    F = w_up.shape[1]
    t = PAD
    lead_blocks = PAD // t
    first = pl.BlockSpec((None, t, D), lambda b, i: (b, 2 * i + lead_blocks, 0))
    second = pl.BlockSpec((None, t, D), lambda b, i: (b, 2 * i + lead_blocks + 1, 0))
    return pl.pallas_call(
        functools.partial(_out_mlp_final_body, f_chunk=MLP_F_CHUNK),
        grid=(B, (Lp - PAD) // (2 * t)),
        in_specs=[first, second, first, second,
                  _resident((D, D)),
                  _resident((1, D)),
                  _resident((D, F)),
                  _resident((F, D)),
                  _resident((1, D))],
        out_specs=pl.BlockSpec((None, 2 * t, D), lambda b, i: (b, i, 0)),
        out_shape=jax.ShapeDtypeStruct((B, Lp - PAD, D), F32),
        compiler_params=pltpu.CompilerParams(
            dimension_semantics=("parallel", "parallel"), vmem_limit_bytes=VMEM_LIMIT),
        name="out_mlp_final",
    )(h, h, o, o, w_out, norm_w.reshape(1, D), w_up, w_down, final_w.reshape(1, D))


def _lambda_init_for(layer_idx):
    return 0.8 - 0.6 * math.exp(-0.3 * layer_idx)


def kernel(x, meta_tokens, mix_norm_w, attn_w_in, attn_lambda, attn_subln_w, attn_w_out, gla_w_in, gla_w_gate_up, gla_gate_bias, gla_norm_w, gla_w_out, mlp_norm_w, mlp_w_up, mlp_w_down, final_norm_w):
    B, seq, D = x.shape
    Lp = PAD + seq
    T = B * Lp
    lead = jnp.concatenate([jnp.zeros((FIRST_KEY, D), x.dtype), meta_tokens.astype(x.dtype)], axis=0)

    for i in range(DEPTH):
        j = i // 2
        if i % 2 == 0:
            col_scale = jnp.where(jnp.arange(3 * D) < D, ATTN_Q_SCALE, 1.0).astype(F32)
            w_in = (attn_w_in[j] * col_scale).astype(BF16)
            if i == 0:
                h, q_t, kv = norm_proj_first(x, lead, mix_norm_w[i], w_in)
            else:
                q_t, kv = norm_proj(h, mix_norm_w[i], w_in)
            o = diff_attention(q_t, kv.reshape(B, Lp, 2 * D), attn_lambda[j], attn_subln_w[j],
                               _lambda_init_for(i))
            w_out = attn_w_out[j]
        else:
            n_main = 2 * GLA_KEY_DIM + 2 * GLA_VAL_DIM
            w_in = gla_w_in[j]
            w_gz = jnp.pad(w_in[:, n_main:], ((0, 0), (0, LANE - GLA_GATE_RANK))).astype(BF16)
            w_gu = jnp.pad(gla_w_gate_up[j], ((0, LANE - GLA_GATE_RANK), (0, 0))).astype(BF16)
            q_scale = jnp.where(jnp.arange(n_main) < GLA_KEY_DIM, GLA_HK ** -0.5, 1.0).astype(F32)
            qkvg, glog = gla_proj(h, mix_norm_w[i], (w_in[:, :n_main] * q_scale).astype(BF16), w_gz, w_gu,
                                  gla_gate_bias[j])
            o = gla_mixer(qkvg.reshape(B, Lp, n_main), glog.reshape(B, Lp, GLA_KEY_DIM), gla_norm_w[j])
            w_out = gla_w_out[j]
        mlp_args = (w_out.astype(BF16), mlp_norm_w[i], mlp_w_up[i].astype(BF16), mlp_w_down[i].astype(BF16))
        if i < DEPTH - 1:
            h = out_mlp(h, o.reshape(T, D), *mlp_args)
    return out_mlp_final(h.reshape(B, Lp, D), o.reshape(B, Lp, D), *mlp_args, final_norm_w)
```

```python
import functools
import math

import numpy as np
import jax
import jax.numpy as jnp
from jax import lax
from jax.experimental import pallas as pl
from jax.experimental.pallas import tpu as pltpu

F32 = jnp.float32
BF16 = jnp.bfloat16

DEPTH = 4
N_META = 16
ATTN_TQ = 256
PAD = ATTN_TQ
FIRST_KEY = PAD - N_META
EPS = 1e-6
NEG = -1e30
LOG2E = math.log2(math.e)

DA_HEADS = 8
DA_HEAD_DIM = 64
DA_V_DIM = 128
GLA_HEADS = 4
GLA_HK = 128
GLA_HV = 256
GLA_KEY_DIM = 512
GLA_VAL_DIM = 1024
GLA_GATE_RANK = 16
GLA_GATE_NORM = 16.0
GLA_CHUNK = 128
GLA_LEVELS = 7
GLA_CHUNKS_PER_STEP = 2
GLA_SEQS_PER_STEP = 2
PROJ_N_CHUNK = 512
MLP_F_CHUNK = 1024

LANE = 128
VMEM_LIMIT = 50 * 1024 * 1024
ATTN_VMEM_LIMIT = 57 * 1024 * 1024

NT_DIMS = (((1,), (1,)), ((), ()))
TN_DIMS = (((0,), (0,)), ((), ()))


def _row_tile(n_rows):
    for t in (512, 384, 256, 128):
        if n_rows % t == 0:
            return t
    raise ValueError(f"row count {n_rows} is not a multiple of 128")


def _rms_scale(x, w):
    return x * lax.rsqrt(jnp.mean(x * x, axis=-1, keepdims=True) + EPS) * w


def _resident(shape):
    return pl.BlockSpec(shape, lambda *_: (0,) * len(shape), pipeline_mode=pl.Buffered(1))


def _norm_proj_first_body(x_ref, lead_ref, nw_ref, w_ref, h_ref, qt_ref, kv_ref, *, n_chunk):
    @pl.when(pl.program_id(1) == 0)
    def _():
        h_ref[...] = lead_ref[...]

    @pl.when(pl.program_id(1) > 0)
    def _():
        h_ref[...] = x_ref[...]

    _norm_proj_body(h_ref, nw_ref, w_ref, qt_ref, kv_ref, n_chunk=n_chunk)


def norm_proj_first(x, lead, norm_w, w):
    B, seq, D = x.shape
    N = w.shape[1]
    t = PAD
    n = (PAD + seq) // t
    T = B * n * t
    return pl.pallas_call(
        functools.partial(_norm_proj_first_body, n_chunk=PROJ_N_CHUNK),
        grid=(B, n),
        in_specs=[pl.BlockSpec((None, t, D), lambda b, r: (b, jnp.maximum(r - 1, 0), 0)),
                  _resident((t, D)),
                  _resident((1, D)),
                  _resident((D, N))],
        out_specs=[pl.BlockSpec((t, D), lambda b, r: (b * n + r, 0)),
                   pl.BlockSpec((D, t), lambda b, r: (0, b * n + r)),
                   pl.BlockSpec((t, N - D), lambda b, r: (b * n + r, 0))],
        out_shape=[jax.ShapeDtypeStruct((T, D), F32),
                   jax.ShapeDtypeStruct((D, T), BF16),
                   jax.ShapeDtypeStruct((T, N - D), BF16)],
        compiler_params=pltpu.CompilerParams(
            dimension_semantics=("parallel", "parallel"), vmem_limit_bytes=VMEM_LIMIT),
        name="norm_proj_first",
    )(x, lead, norm_w.reshape(1, D), w)


def _norm_proj_body(h_ref, nw_ref, w_ref, qt_ref, kv_ref, *, n_chunk):
    hn = _rms_scale(h_ref[...], nw_ref[...]).astype(BF16)
    n_q = qt_ref.shape[0]
    for c in range(0, n_q, n_chunk):
        q = jnp.dot(hn, w_ref[:, c:c + n_chunk], preferred_element_type=F32)
        qt_ref[c:c + n_chunk, :] = q.T.astype(qt_ref.dtype)
    for c in range(n_q, w_ref.shape[1], n_chunk):
        kv_ref[:, c - n_q:c - n_q + n_chunk] = jnp.dot(
            hn, w_ref[:, c:c + n_chunk], preferred_element_type=F32).astype(kv_ref.dtype)


def norm_proj(h, norm_w, w):
    T, D = h.shape
    N = w.shape[1]
    tm = _row_tile(T)
    return pl.pallas_call(
        functools.partial(_norm_proj_body, n_chunk=PROJ_N_CHUNK),
        grid=(T // tm,),
        in_specs=[pl.BlockSpec((tm, D), lambda i: (i, 0)),
                  _resident((1, D)),
                  _resident((D, N))],
        out_specs=[pl.BlockSpec((D, tm), lambda i: (0, i)),
                   pl.BlockSpec((tm, N - D), lambda i: (i, 0))],
        out_shape=[jax.ShapeDtypeStruct((D, T), BF16),
                   jax.ShapeDtypeStruct((T, N - D), BF16)],
        compiler_params=pltpu.CompilerParams(
            dimension_semantics=("parallel",), vmem_limit_bytes=VMEM_LIMIT),
        name="norm_proj",
    )(h, norm_w.reshape(1, D), w)


def _gla_proj_body(h_ref, nw_ref, w_ref, wgz_ref, wgu_ref, gb_ref, o_ref, gl_ref, *, n_chunk):
    hn = _rms_scale(h_ref[...], nw_ref[...]).astype(BF16)
    for c in range(0, w_ref.shape[1], n_chunk):
        o_ref[:, c:c + n_chunk] = jnp.dot(
            hn, w_ref[:, c:c + n_chunk], preferred_element_type=F32).astype(o_ref.dtype)
    gz = jnp.dot(hn, wgz_ref[...], preferred_element_type=F32).astype(BF16)
    z = jnp.dot(gz, wgu_ref[...], preferred_element_type=F32) + gb_ref[...]
    gl_ref[...] = (jnp.minimum(z, 0.0) - jnp.log1p(jnp.exp(-jnp.abs(z)))) * (LOG2E / GLA_GATE_NORM)


def gla_proj(h, norm_w, w_main, w_gz, w_gate_up, gate_bias):
    T, D = h.shape
    N = w_main.shape[1]
    tm = _row_tile(T)
    return pl.pallas_call(
        functools.partial(_gla_proj_body, n_chunk=PROJ_N_CHUNK),
        grid=(T // tm,),
        in_specs=[pl.BlockSpec((tm, D), lambda i: (i, 0)),
                  _resident((1, D)),
                  _resident((D, N)),
                  _resident((D, LANE)),
                  _resident((LANE, GLA_KEY_DIM)),
                  _resident((1, GLA_KEY_DIM))],
        out_specs=[pl.BlockSpec((tm, N), lambda i: (i, 0)),
                   pl.BlockSpec((tm, GLA_KEY_DIM), lambda i: (i, 0))],
        out_shape=[jax.ShapeDtypeStruct((T, N), BF16),
                   jax.ShapeDtypeStruct((T, GLA_KEY_DIM), F32)],
        compiler_params=pltpu.CompilerParams(
            dimension_semantics=("parallel",), vmem_limit_bytes=VMEM_LIMIT),
        name="gla_proj",
    )(h, norm_w.reshape(1, D), w_main, w_gz, w_gate_up, gate_bias.reshape(1, GLA_KEY_DIM))


ATTN_HEADS_PER_STEP = 8
ATTN_ACC_ROWS = DA_V_DIM + 16
SLOPE_PIECES = 3
ATTN_Q_SCALE = DA_HEAD_DIM ** -0.5 * LOG2E


def _attn_body(slopes_ref, q_ref, k_ref, v_ref, lam_ref, sub_ref, o_ref, sa_ref, sb_ref, acc_ref, *,
               n_tiles, lambda_init):
    tq = ATTN_TQ
    hp = pl.program_id(1)
    qi = pl.program_id(2)
    row0 = pl.multiple_of(qi * tq, tq)
    lane = lax.broadcasted_iota(jnp.int32, (tq, LANE), 1)
    sub = lax.broadcasted_iota(jnp.int32, (tq, LANE), 0)
    pos = jnp.where(lane < SLOPE_PIECES, sub, 0).astype(BF16)
    ones_rows = jnp.ones((ATTN_ACC_ROWS - DA_V_DIM, tq), BF16)

    dim_i = lax.broadcasted_iota(jnp.int32, (DA_V_DIM, tq), 0)
    feat_i = lax.broadcasted_iota(jnp.int32, (LANE, 2 * tq), 0)
    heads = []
    for hh in range(ATTN_HEADS_PER_STEP):
        cs = slice(hh * DA_V_DIM, (hh + 1) * DA_V_DIM)
        h = ATTN_HEADS_PER_STEP * hp + hh
        q = q_ref[cs, :].astype(F32)
        feat = jnp.zeros((LANE, 2 * tq), F32)
        for j in range(SLOPE_PIECES):
            feat = jnp.where(feat_i == j, slopes_ref[h, j], feat)
        q_t = jnp.concatenate(
            [jnp.concatenate([jnp.where(dim_i < DA_HEAD_DIM, q, 0.0),
                              jnp.where(dim_i >= DA_HEAD_DIM, q, 0.0)], axis=1), feat], axis=0)
        heads.append((cs, slopes_ref[h, SLOPE_PIECES], q_t.astype(BF16)))

    def scores(head, start):
        k_aug = jnp.concatenate([k_ref[pl.ds(start, tq), head[0]], pos], axis=1)
        s = jnp.dot(k_aug, head[2], preferred_element_type=F32)
        return s, jnp.max(s, axis=0, keepdims=True)

    def update(hh, m, s, s_max, c, pv_and_sum):
        m_new = jnp.maximum(m, s_max + c)
        alpha = jnp.exp2(m - m_new)
        p = jnp.exp2(s - (m_new - c))
        acc_ref[hh] = alpha * acc_ref[hh] + pv_and_sum(p)
        return m_new

    def tile_start(t):
        return pl.multiple_of(t * tq, tq)

    def full_tile(hh, m, s, s_max, start):
        c = heads[hh][1] * (start - row0).astype(F32)
        v_t = jnp.concatenate([v_ref[pl.ds(start, tq), heads[hh][0]].T, ones_rows], axis=0)
        return update(hh, m, s, s_max, c,
                      lambda p: jnp.dot(v_t, p.astype(BF16), preferred_element_type=F32))

    n_full = qi - 1
    trips = jnp.maximum(n_full - 1, 0) // 2
    carries = []
    for hh, head in enumerate(heads):
        acc_ref[hh] = jnp.zeros((ATTN_ACC_ROWS, 2 * tq), F32)
        s, s_max = scores(head, tile_start(1))
        sa_ref[hh] = s
        carries.append((jnp.full((1, 2 * tq), NEG, F32), s_max))

    def issue(buf_ref, t):
        maxes = []
        for hh, head in enumerate(heads):
            s, s_max = scores(head, tile_start(t))
            buf_ref[hh] = s
            maxes.append(s_max)
        return maxes

    def consume(buf_ref, t, ms, maxes, off=None):
        out = []
        for hh, (m, s_max) in enumerate(zip(ms, maxes)):
            s = buf_ref[hh]
            if off is not None:
                s, s_max = s + off, s_max + off
            out.append(full_tile(hh, m, s, s_max, tile_start(t)))
        return out

    def body(i, carries):
        t = 1 + 2 * i
        max_b = issue(sb_ref, t + 1)
        ms = consume(sa_ref, t, [c[0] for c in carries], [c[1] for c in carries])
        max_a = issue(sa_ref, t + 2)
        ms = consume(sb_ref, t + 1, ms, max_b)
        return tuple(zip(ms, max_a))

    carries = lax.fori_loop(0, trips, body, tuple(carries))

    t_a = 1 + 2 * trips
    remaining = n_full - 2 * trips
    ms = consume(sa_ref, t_a, [c[0] for c in carries], [c[1] for c in carries],
                 jnp.where(remaining >= 1, 0.0, NEG))

    def second_tile(ms):
        return tuple(consume(sb_ref, t_a + 1, list(ms), issue(sb_ref, t_a + 1)))

    ms = lax.cond(remaining >= 2, second_tile, lambda ms: ms, tuple(ms))

    half = tq // 2
    key_i = lax.broadcasted_iota(jnp.int32, (N_META + half, 2 * tq), 0) - N_META
    qry_i = lax.broadcasted_iota(jnp.int32, (N_META + half, 2 * tq), 1) & (tq - 1)
    meta_base = jnp.where(qi >= 1, PAD, -PAD)
    key_row = jnp.where(key_i < 0, meta_base + key_i, row0 + key_i)
    mask_1 = (key_row <= row0 + qry_i) & (key_row >= FIRST_KEY)
    key_2 = half + lax.broadcasted_iota(jnp.int32, (half, tq), 0)
    qry_2 = half + (lax.broadcasted_iota(jnp.int32, (half, tq), 1) & (half - 1))
    mask_2 = (key_2 <= qry_2) & (row0 + key_2 >= FIRST_KEY)
    pos_1 = jnp.concatenate([pos[:N_META], pos[:half]], axis=0)
    late = [slice(half, tq), slice(tq + half, 2 * tq)]
    lp = lam_ref[...]
    lam = (jnp.exp(jnp.sum(lp[0:1] * lp[1:2], axis=1, keepdims=True))
           - jnp.exp(jnp.sum(lp[2:3] * lp[3:4], axis=1, keepdims=True)) + lambda_init)
    row = row0 + lax.broadcasted_iota(jnp.int32, (tq, 1), 0)
    n_sum = ATTN_ACC_ROWS - DA_V_DIM

    def values_and_sums(v, p):
        return jnp.concatenate(
            [lax.dot_general(v, p.astype(BF16), TN_DIMS, preferred_element_type=F32),
             jnp.broadcast_to(jnp.sum(p, axis=0, keepdims=True), (n_sum, p.shape[1]))], axis=0)

    for hh, (head, m) in enumerate(zip(heads, ms)):
        cs, slope, q_t = head
        k_1 = jnp.concatenate([k_ref[pl.ds(FIRST_KEY, N_META), cs], k_ref[pl.ds(row0, half), cs]], axis=0)
        v_1 = jnp.concatenate([v_ref[pl.ds(FIRST_KEY, N_META), cs], v_ref[pl.ds(row0, half), cs]], axis=0)
        s_1 = jnp.dot(jnp.concatenate([k_1, pos_1], axis=1), q_t, preferred_element_type=F32)
        s_1 = s_1 + jnp.where(key_i[:, :1] < 0, slope * (FIRST_KEY - row0).astype(F32), 0.0)
        s_1 = jnp.where(mask_1, s_1, NEG)
        m = update(hh, m, s_1, jnp.max(s_1, axis=0, keepdims=True), 0.0,
                   functools.partial(values_and_sums, v_1))
        start_2 = pl.multiple_of(row0 + half, half)
        k_2 = jnp.concatenate([k_ref[pl.ds(start_2, half), cs], pos[half:]], axis=1)
        q_2 = jnp.concatenate([q_t[:, c] for c in late], axis=1)
        s_2 = jnp.where(mask_2, jnp.dot(k_2, q_2, preferred_element_type=F32), NEG)
        m_2 = jnp.concatenate([m[:, c] for c in late], axis=1)
        m_new = jnp.maximum(m_2, jnp.max(s_2, axis=0, keepdims=True))
        alpha = jnp.exp2(m_2 - m_new)
        pv_2 = values_and_sums(v_ref[pl.ds(start_2, half), cs], jnp.exp2(s_2 - m_new))
        for j, c in enumerate(late):
            jc = slice(j * half, (j + 1) * half)
            acc_ref[hh, :, c] = alpha[:, jc] * acc_ref[hh, :, c] + pv_2[:, jc]
        acc = acc_ref[hh]
        o_t = acc[:DA_V_DIM] * (1.0 / acc[DA_V_DIM:DA_V_DIM + 1])
        d = (o_t[:, :tq] - lam * o_t[:, tq:]).T
        d = jnp.where(row >= FIRST_KEY, d, 0.0)
        y = _rms_scale(d, sub_ref[...]) * (1.0 - lambda_init)
        o_ref[:, cs] = y.astype(o_ref.dtype)


def diff_attention(q_t, kv, lam_params, subln_w, lambda_init):
    B, Lp, _ = kv.shape
    hps = ATTN_HEADS_PER_STEP
    w = hps * DA_V_DIM
    n_col = DA_HEADS // hps
    n_tiles = Lp // ATTN_TQ
    slope = 2.0 ** (-8.0 * jnp.arange(1, DA_HEADS + 1, dtype=F32) / DA_HEADS) * LOG2E
    pieces, rest = [], slope
    for _ in range(SLOPE_PIECES):
        pieces.append(rest.astype(BF16).astype(F32))
        rest = rest - pieces[-1]
    slopes = jnp.stack(pieces + [slope], axis=1)
    kv_spec = lambda sec: pl.BlockSpec((None, Lp, w), lambda b, h, i: (b, 0, sec * n_col + h))
    return pl.pallas_call(
        functools.partial(_attn_body, n_tiles=n_tiles, lambda_init=lambda_init),
        grid=(B, n_col, n_tiles),
        in_specs=[pl.BlockSpec(memory_space=pltpu.SMEM),
                  pl.BlockSpec((w, ATTN_TQ), lambda b, h, i: (h, b * n_tiles + i)),
                  kv_spec(0),
                  kv_spec(1),
                  _resident((4, DA_HEAD_DIM)),
                  _resident((1, DA_V_DIM))],
        out_specs=pl.BlockSpec((None, ATTN_TQ, w), lambda b, h, i: (b, i, h)),
        out_shape=jax.ShapeDtypeStruct((B, Lp, DA_HEADS * DA_V_DIM), BF16),
        scratch_shapes=[pltpu.VMEM((hps, ATTN_TQ, 2 * ATTN_TQ), F32),
                        pltpu.VMEM((hps, ATTN_TQ, 2 * ATTN_TQ), F32),
                        pltpu.VMEM((hps, ATTN_ACC_ROWS, 2 * ATTN_TQ), F32)],
        compiler_params=pltpu.CompilerParams(
            dimension_semantics=("parallel", "parallel", "arbitrary"), vmem_limit_bytes=ATTN_VMEM_LIMIT),
        name="diff_attention",
    )(slopes, q_t, kv, kv, lam_params.astype(F32), subln_w.reshape(1, DA_V_DIM).astype(F32))


def _gla_tables():
    C = GLA_CHUNK
    i = np.arange(C)[:, None]
    t = np.arange(C)[None, :]
    blocks = [t <= i, t > i]
    level = np.full((C, C), GLA_LEVELS + 1, np.int32)
    level[np.arange(C), np.arange(C)] = GLA_LEVELS
    for l in range(GLA_LEVELS):
        s = C >> (l + 1)
        m = (i // (2 * s)) * 2 * s + s - 1
        upper = (i & s) != 0
        blocks.append(np.where(upper, (t > m) & (t <= i), (t > i) & (t <= m)))
        level[(i > t) & (((i ^ t) >> (GLA_LEVELS - 1 - l)) == 1)] = l
    sums = np.concatenate(blocks, axis=0).astype(np.float32)
    sums = np.concatenate([sums, sums], axis=1)
    return jnp.asarray(sums, BF16), jnp.asarray(level)


def _gla_body(level_ref, sums_ref, q_ref, k_ref, v_ref, g_ref, gl_ref, nw_ref, o_ref, st_ref, *, n_seqs):
    C = GLA_CHUNK

    @pl.when(pl.program_id(1) == 0)
    def _():
        st_ref[...] = jnp.zeros_like(st_ref)

    level = level_ref[...]
    at_level = [level == l for l in range(GLA_LEVELS + 1)]
    for cc in range(GLA_CHUNKS_PER_STEP):
        for seq in range(n_seqs):
            _gla_chunk(seq, slice(cc * C, (cc + 1) * C), at_level, sums_ref, q_ref, k_ref, v_ref, g_ref,
                       gl_ref, nw_ref, o_ref, st_ref)


def _gla_chunk(seq, rows, at_level, sums_ref, q_ref, k_ref, v_ref, g_ref, gl_ref, nw_ref, o_ref, st_ref):
    C = GLA_CHUNK
    gl = gl_ref[seq, rows, :]
    g_hi = gl.astype(BF16)
    g_lo = (gl - g_hi.astype(F32)).astype(BF16)
    w_all = jnp.dot(sums_ref[...], jnp.concatenate([g_hi, g_lo], axis=0),
                    preferred_element_type=F32)

    key_cols = [slice(h * GLA_HK, (h + 1) * GLA_HK) for h in range(GLA_HEADS)]
    qs = [q_ref[seq, rows, ks].astype(F32) for ks in key_cols]
    ks_ = [k_ref[seq, rows, ks].astype(F32) for ks in key_cols]
    ws = [w_all[:, ks] for ks in key_cols]
    scores = [jnp.where(at_level[GLA_LEVELS], jnp.sum(q * k, axis=1, keepdims=True), 0.0)
              for q, k in zip(qs, ks_)]
    zero = jnp.zeros((C, GLA_HK), BF16)
    for h0 in range(0, GLA_HEADS, 2):
        for l in range(GLA_LEVELS):
            e = [jnp.exp2(ws[h][(2 + l) * C:(3 + l) * C]) for h in (h0, h0 + 1)]
            qe2 = jnp.concatenate([(qs[h0] * e[0]).astype(BF16), (qs[h0 + 1] * e[1]).astype(BF16)], axis=1)
            ke2 = jnp.concatenate(
                [jnp.concatenate([(ks_[h0] * e[0]).astype(BF16), zero], axis=1),
                 jnp.concatenate([zero, (ks_[h0 + 1] * e[1]).astype(BF16)], axis=1)], axis=0)
            a_l = lax.dot_general(qe2, ke2, NT_DIMS, preferred_element_type=F32)
            scores[h0] = jnp.where(at_level[l], a_l[:, :C], scores[h0])
            scores[h0 + 1] = jnp.where(at_level[l], a_l[:, C:], scores[h0 + 1])

    for h in range(GLA_HEADS):
        vs = slice(h * GLA_HV, (h + 1) * GLA_HV)
        q, k, w, a = qs[h], ks_[h], ws[h], scores[h]
        v = v_ref[seq, rows, vs]
        cum = w[0:C]
        state_t = st_ref[seq, h]
        qe = (q * jnp.exp2(cum)).astype(BF16)
        o = (jnp.dot(a.astype(BF16), v, preferred_element_type=F32)
             + lax.dot_general(qe, state_t.astype(BF16), NT_DIMS, preferred_element_type=F32))
        kd = (k * jnp.exp2(w[C:2 * C])).astype(BF16)
        st_ref[seq, h] = (state_t * jnp.exp2(cum[C - 1:C])
                     + lax.dot_general(v, kd, TN_DIMS, preferred_element_type=F32))
        g = g_ref[seq, rows, vs].astype(F32)
        y = _rms_scale(o, nw_ref[...]) * (g / (1.0 + jnp.exp(-g)))
        o_ref[seq, rows, vs] = y.astype(o_ref.dtype)


def gla_mixer(qkvg, glog, norm_w):
    B, Lp, _ = qkvg.shape
    C = GLA_CHUNK
    sums, level = _gla_tables()
    kw, vw = GLA_KEY_DIM, GLA_VAL_DIM
    R = GLA_CHUNKS_PER_STEP * C
    S = GLA_SEQS_PER_STEP if B % GLA_SEQS_PER_STEP == 0 else 1
    return pl.pallas_call(
        functools.partial(_gla_body, n_seqs=S),
        grid=(B // S, Lp // R),
        in_specs=[_resident((C, C)),
                  _resident(((2 + GLA_LEVELS) * C, 2 * C)),
                  pl.BlockSpec((S, R, kw), lambda b, c: (b, c, 0)),
                  pl.BlockSpec((S, R, kw), lambda b, c: (b, c, 1)),
                  pl.BlockSpec((S, R, vw), lambda b, c: (b, c, 1)),
                  pl.BlockSpec((S, R, vw), lambda b, c: (b, c, 2)),
                  pl.BlockSpec((S, R, kw), lambda b, c: (b, c, 0)),
                  _resident((1, GLA_HV))],
        out_specs=pl.BlockSpec((S, R, vw), lambda b, c: (b, c, 0)),
        out_shape=jax.ShapeDtypeStruct((B, Lp, vw), BF16),
        scratch_shapes=[pltpu.VMEM((S, GLA_HEADS, GLA_HV, GLA_HK), F32)],
        compiler_params=pltpu.CompilerParams(
            dimension_semantics=("parallel", "arbitrary"), vmem_limit_bytes=VMEM_LIMIT),
        name="gla_mixer",
    )(level, sums, qkvg, qkvg, qkvg, qkvg, glog, norm_w.reshape(1, GLA_HV).astype(F32))


def _mlp_rows(h, o, wo_ref, nw_ref, wu_ref, wd_ref, out_ref, f_chunk):
    h1 = h + jnp.dot(o, wo_ref[...], preferred_element_type=F32)
    hn = _rms_scale(h1, nw_ref[...]).astype(BF16)
    out_ref[...] = h1
    for c in range(0, wu_ref.shape[1], f_chunk):
        u = jnp.dot(hn, wu_ref[:, c:c + f_chunk], preferred_element_type=F32)
        u = jnp.square(jnp.maximum(u, 0.0)).astype(BF16)
        out_ref[...] += jnp.dot(u, wd_ref[c:c + f_chunk, :], preferred_element_type=F32)


def _out_mlp_body(h_ref, o_ref, wo_ref, nw_ref, wu_ref, wd_ref, out_ref, *, f_chunk):
    _mlp_rows(h_ref[...], o_ref[...], wo_ref, nw_ref, wu_ref, wd_ref, out_ref, f_chunk)


def _out_mlp_final_body(ha_ref, hb_ref, oa_ref, ob_ref, wo_ref, nw_ref, wu_ref, wd_ref, fw_ref, out_ref, *,
                        f_chunk):
    h = jnp.concatenate([ha_ref[...], hb_ref[...]], axis=0)
    o = jnp.concatenate([oa_ref[...], ob_ref[...]], axis=0)
    _mlp_rows(h, o, wo_ref, nw_ref, wu_ref, wd_ref, out_ref, f_chunk)
    out_ref[...] = _rms_scale(out_ref[...], fw_ref[...])


def out_mlp(h, o, w_out, norm_w, w_up, w_down):
    T, D = h.shape
    F = w_up.shape[1]
    tm = _row_tile(T)
    return pl.pallas_call(
        functools.partial(_out_mlp_body, f_chunk=MLP_F_CHUNK),
        grid=(T // tm,),
        in_specs=[pl.BlockSpec((tm, D), lambda i: (i, 0)),
                  pl.BlockSpec((tm, D), lambda i: (i, 0)),
                  _resident((D, D)),
                  _resident((1, D)),
                  _resident((D, F)),
                  _resident((F, D))],
        out_specs=pl.BlockSpec((tm, D), lambda i: (i, 0)),
        out_shape=jax.ShapeDtypeStruct((T, D), F32),
        compiler_params=pltpu.CompilerParams(
            dimension_semantics=("parallel",), vmem_limit_bytes=VMEM_LIMIT),
        name="out_mlp",
    )(h, o, w_out, norm_w.reshape(1, D), w_up, w_down)


def _out_mlp_gla_proj_body(h_ref, o_ref, wo_ref, nw_ref, wu_ref, wd_ref, nw2_ref, w_ref, wgz_ref, wgu_ref,
                           gb_ref, out_ref, qkvg_ref, gl_ref, *, f_chunk, n_chunk):
    _mlp_rows(h_ref[...], o_ref[...], wo_ref, nw_ref, wu_ref, wd_ref, out_ref, f_chunk)
    _gla_proj_body(out_ref, nw2_ref, w_ref, wgz_ref, wgu_ref, gb_ref, qkvg_ref, gl_ref, n_chunk=n_chunk)


def out_mlp_gla_proj(h, o, w_out, norm_w, w_up, w_down, norm_w2, w_main, w_gz, w_gate_up, gate_bias):
    T, D = h.shape
    F = w_up.shape[1]
    N = w_main.shape[1]
    tm = _row_tile(T)
    rows = lambda n: pl.BlockSpec((tm, n), lambda i: (i, 0))
    return pl.pallas_call(
        functools.partial(_out_mlp_gla_proj_body, f_chunk=MLP_F_CHUNK, n_chunk=PROJ_N_CHUNK),
        grid=(T // tm,),
        in_specs=[rows(D), rows(D),
                  _resident((D, D)),
                  _resident((1, D)),
                  _resident((D, F)),
                  _resident((F, D)),
                  _resident((1, D)),
                  _resident((D, N)),
                  _resident((D, LANE)),
                  _resident((LANE, GLA_KEY_DIM)),
                  _resident((1, GLA_KEY_DIM))],
        out_specs=[rows(D), rows(N), rows(GLA_KEY_DIM)],
        out_shape=[jax.ShapeDtypeStruct((T, D), F32),
                   jax.ShapeDtypeStruct((T, N), BF16),
                   jax.ShapeDtypeStruct((T, GLA_KEY_DIM), F32)],
        compiler_params=pltpu.CompilerParams(
            dimension_semantics=("parallel",), vmem_limit_bytes=ATTN_VMEM_LIMIT),
        name="out_mlp_gla_proj",
    )(h, o, w_out, norm_w.reshape(1, D), w_up, w_down, norm_w2.reshape(1, D), w_main, w_gz, w_gate_up,
      gate_bias.reshape(1, GLA_KEY_DIM))


def out_mlp_final(h, o, w_out, norm_w, w_up, w_down, final_w):
    B, Lp, D = h.shape
    F = w_up.shape[1]
    t = PAD
    lead_blocks = PAD // t
    first = pl.BlockSpec((None, t, D), lambda b, i: (b, 2 * i + lead_blocks, 0))
    second = pl.BlockSpec((None, t, D), lambda b, i: (b, 2 * i + lead_blocks + 1, 0))
    return pl.pallas_call(
        functools.partial(_out_mlp_final_body, f_chunk=MLP_F_CHUNK),
        grid=(B, (Lp - PAD) // (2 * t)),
        in_specs=[first, second, first, second,
                  _resident((D, D)),
                  _resident((1, D)),
                  _resident((D, F)),
                  _resident((F, D)),
                  _resident((1, D))],
        out_specs=pl.BlockSpec((None, 2 * t, D), lambda b, i: (b, i, 0)),
        out_shape=jax.ShapeDtypeStruct((B, Lp - PAD, D), F32),
        compiler_params=pltpu.CompilerParams(
            dimension_semantics=("parallel", "parallel"), vmem_limit_bytes=VMEM_LIMIT),
        name="out_mlp_final",
    )(h, h, o, o, w_out, norm_w.reshape(1, D), w_up, w_down, final_w.reshape(1, D))


def _lambda_init_for(layer_idx):
    return 0.8 - 0.6 * math.exp(-0.3 * layer_idx)


def kernel(x, meta_tokens, mix_norm_w, attn_w_in, attn_lambda, attn_subln_w, attn_w_out, gla_w_in, gla_w_gate_up, gla_gate_bias, gla_norm_w, gla_w_out, mlp_norm_w, mlp_w_up, mlp_w_down, final_norm_w):
    B, seq, D = x.shape
    Lp = PAD + seq
    T = B * Lp
    lead = jnp.concatenate([jnp.zeros((FIRST_KEY, D), x.dtype), meta_tokens.astype(x.dtype)], axis=0)

    for i in range(DEPTH):
        j = i // 2
        if i % 2 == 0:
            col_scale = jnp.where(jnp.arange(3 * D) < D, ATTN_Q_SCALE, 1.0).astype(F32)
            w_in = (attn_w_in[j] * col_scale).astype(BF16)
            if i == 0:
                h, q_t, kv = norm_proj_first(x, lead, mix_norm_w[i], w_in)
            else:
                q_t, kv = norm_proj(h, mix_norm_w[i], w_in)
            o = diff_attention(q_t, kv.reshape(B, Lp, 2 * D), attn_lambda[j], attn_subln_w[j],
                               _lambda_init_for(i))
            w_out = attn_w_out[j]
        else:
            o = gla_mixer(qkvg.reshape(B, Lp, n_main), glog.reshape(B, Lp, GLA_KEY_DIM), gla_norm_w[j])
            w_out = gla_w_out[j]
        mlp_args = (w_out.astype(BF16), mlp_norm_w[i], mlp_w_up[i].astype(BF16), mlp_w_down[i].astype(BF16))
        if i % 2 == 0:
            n_main = 2 * GLA_KEY_DIM + 2 * GLA_VAL_DIM
            w_in = gla_w_in[j]
            w_gz = jnp.pad(w_in[:, n_main:], ((0, 0), (0, LANE - GLA_GATE_RANK))).astype(BF16)
            w_gu = jnp.pad(gla_w_gate_up[j], ((0, LANE - GLA_GATE_RANK), (0, 0))).astype(BF16)
            q_scale = jnp.where(jnp.arange(n_main) < GLA_KEY_DIM, GLA_HK ** -0.5, 1.0).astype(F32)
            h, qkvg, glog = out_mlp_gla_proj(
                h, o.reshape(T, D), *mlp_args, mix_norm_w[i + 1],
                (w_in[:, :n_main] * q_scale).astype(BF16), w_gz, w_gu, gla_gate_bias[j])
        elif i < DEPTH - 1:
            h = out_mlp(h, o.reshape(T, D), *mlp_args)
    return out_mlp_final(h.reshape(B, Lp, D), o.reshape(B, Lp, D), *mlp_args, final_norm_w)
```

```python
import functools
import math

import numpy as np
import jax
import jax.numpy as jnp
from jax import lax
from jax.experimental import pallas as pl
from jax.experimental.pallas import tpu as pltpu

F32 = jnp.float32
BF16 = jnp.bfloat16

DEPTH = 4
N_META = 16
ATTN_TQ = 256
PAD = ATTN_TQ
FIRST_KEY = PAD - N_META
EPS = 1e-6
NEG = -1e30
LOG2E = math.log2(math.e)

DA_HEADS = 8
DA_HEAD_DIM = 64
DA_V_DIM = 128
GLA_HEADS = 4
GLA_HK = 128
GLA_HV = 256
GLA_KEY_DIM = 512
GLA_VAL_DIM = 1024
GLA_GATE_RANK = 16
GLA_GATE_NORM = 16.0
GLA_CHUNK = 128
GLA_LEVELS = 7
GLA_CHUNKS_PER_STEP = 2
GLA_SEQS_PER_STEP = 2
PROJ_N_CHUNK = 512
MLP_F_CHUNK = 1024

LANE = 128
VMEM_LIMIT = 50 * 1024 * 1024
ATTN_VMEM_LIMIT = 57 * 1024 * 1024

NT_DIMS = (((1,), (1,)), ((), ()))
TN_DIMS = (((0,), (0,)), ((), ()))


def _row_tile(n_rows):
    for t in (512, 384, 256, 128):
        if n_rows % t == 0:
            return t
    raise ValueError(f"row count {n_rows} is not a multiple of 128")


def _rms_scale(x, w):
    return x * lax.rsqrt(jnp.mean(x * x, axis=-1, keepdims=True) + EPS) * w


def _resident(shape):
    return pl.BlockSpec(shape, lambda *_: (0,) * len(shape), pipeline_mode=pl.Buffered(1))


def _norm_proj_first_body(x_ref, lead_ref, nw_ref, w_ref, h_ref, qt_ref, kv_ref, *, n_chunk):
    @pl.when(pl.program_id(1) == 0)
    def _():
        h_ref[...] = lead_ref[...]

    @pl.when(pl.program_id(1) > 0)
    def _():
        h_ref[...] = x_ref[...]

    _norm_proj_body(h_ref, nw_ref, w_ref, qt_ref, kv_ref, n_chunk=n_chunk)


def norm_proj_first(x, lead, norm_w, w):
    B, seq, D = x.shape
    N = w.shape[1]
    t = PAD
    n = (PAD + seq) // t
    T = B * n * t
    return pl.pallas_call(
        functools.partial(_norm_proj_first_body, n_chunk=PROJ_N_CHUNK),
        grid=(B, n),
        in_specs=[pl.BlockSpec((None, t, D), lambda b, r: (b, jnp.maximum(r - 1, 0), 0)),
                  _resident((t, D)),
                  _resident((1, D)),
                  _resident((D, N))],
        out_specs=[pl.BlockSpec((t, D), lambda b, r: (b * n + r, 0)),
                   pl.BlockSpec((D, t), lambda b, r: (0, b * n + r)),
                   pl.BlockSpec((t, N - D), lambda b, r: (b * n + r, 0))],
        out_shape=[jax.ShapeDtypeStruct((T, D), F32),
                   jax.ShapeDtypeStruct((D, T), BF16),
                   jax.ShapeDtypeStruct((T, N - D), BF16)],
        compiler_params=pltpu.CompilerParams(
            dimension_semantics=("parallel", "parallel"), vmem_limit_bytes=VMEM_LIMIT),
        name="norm_proj_first",
    )(x, lead, norm_w.reshape(1, D), w)


def _norm_proj_body(h_ref, nw_ref, w_ref, qt_ref, kv_ref, *, n_chunk):
    hn = _rms_scale(h_ref[...], nw_ref[...]).astype(BF16)
    n_q = qt_ref.shape[0]
    for c in range(0, n_q, n_chunk):
        q = jnp.dot(hn, w_ref[:, c:c + n_chunk], preferred_element_type=F32)
        qt_ref[c:c + n_chunk, :] = q.T.astype(qt_ref.dtype)
    for c in range(n_q, w_ref.shape[1], n_chunk):
        kv_ref[:, c - n_q:c - n_q + n_chunk] = jnp.dot(
            hn, w_ref[:, c:c + n_chunk], preferred_element_type=F32).astype(kv_ref.dtype)


def norm_proj(h, norm_w, w):
    T, D = h.shape
    N = w.shape[1]
    tm = _row_tile(T)
    return pl.pallas_call(
        functools.partial(_norm_proj_body, n_chunk=PROJ_N_CHUNK),
        grid=(T // tm,),
        in_specs=[pl.BlockSpec((tm, D), lambda i: (i, 0)),
                  _resident((1, D)),
                  _resident((D, N))],
        out_specs=[pl.BlockSpec((D, tm), lambda i: (0, i)),
                   pl.BlockSpec((tm, N - D), lambda i: (i, 0))],
        out_shape=[jax.ShapeDtypeStruct((D, T), BF16),
                   jax.ShapeDtypeStruct((T, N - D), BF16)],
        compiler_params=pltpu.CompilerParams(
            dimension_semantics=("parallel",), vmem_limit_bytes=VMEM_LIMIT),
        name="norm_proj",
    )(h, norm_w.reshape(1, D), w)


def _gla_proj_body(h_ref, nw_ref, w_ref, wgz_ref, wgu_ref, gb_ref, o_ref, gl_ref, *, n_chunk):
    hn = _rms_scale(h_ref[...], nw_ref[...]).astype(BF16)
    for c in range(0, w_ref.shape[1], n_chunk):
        o_ref[:, c:c + n_chunk] = jnp.dot(
            hn, w_ref[:, c:c + n_chunk], preferred_element_type=F32).astype(o_ref.dtype)
    gz = jnp.dot(hn, wgz_ref[...], preferred_element_type=F32).astype(BF16)
    z = jnp.dot(gz, wgu_ref[...], preferred_element_type=F32) + gb_ref[...]
    gl_ref[...] = (jnp.minimum(z, 0.0) - jnp.log1p(jnp.exp(-jnp.abs(z)))) * (LOG2E / GLA_GATE_NORM)


def gla_proj(h, norm_w, w_main, w_gz, w_gate_up, gate_bias):
    T, D = h.shape
    N = w_main.shape[1]
    tm = _row_tile(T)
    return pl.pallas_call(
        functools.partial(_gla_proj_body, n_chunk=PROJ_N_CHUNK),
        grid=(T // tm,),
        in_specs=[pl.BlockSpec((tm, D), lambda i: (i, 0)),
                  _resident((1, D)),
                  _resident((D, N)),
                  _resident((D, LANE)),
                  _resident((LANE, GLA_KEY_DIM)),
                  _resident((1, GLA_KEY_DIM))],
        out_specs=[pl.BlockSpec((tm, N), lambda i: (i, 0)),
                   pl.BlockSpec((tm, GLA_KEY_DIM), lambda i: (i, 0))],
        out_shape=[jax.ShapeDtypeStruct((T, N), BF16),
                   jax.ShapeDtypeStruct((T, GLA_KEY_DIM), F32)],
        compiler_params=pltpu.CompilerParams(
            dimension_semantics=("parallel",), vmem_limit_bytes=VMEM_LIMIT),
        name="gla_proj",
    )(h, norm_w.reshape(1, D), w_main, w_gz, w_gate_up, gate_bias.reshape(1, GLA_KEY_DIM))


ATTN_HEADS_PER_STEP = 8
ATTN_ACC_ROWS = DA_V_DIM + 16
SLOPE_PIECES = 3
ATTN_Q_SCALE = DA_HEAD_DIM ** -0.5 * LOG2E


def _attn_body(slopes_ref, q_ref, k_ref, v_ref, lam_ref, sub_ref, o_ref, sa_ref, sb_ref, acc_ref, *,
               n_tiles, lambda_init):
    tq = ATTN_TQ
    hp = pl.program_id(1)
    qi = pl.program_id(2)
    row0 = pl.multiple_of(qi * tq, tq)
    lane = lax.broadcasted_iota(jnp.int32, (tq, LANE), 1)
    sub = lax.broadcasted_iota(jnp.int32, (tq, LANE), 0)
    pos = jnp.where(lane < SLOPE_PIECES, sub, 0).astype(BF16)
    ones_rows = jnp.ones((ATTN_ACC_ROWS - DA_V_DIM, tq), BF16)

    dim_i = lax.broadcasted_iota(jnp.int32, (DA_V_DIM, tq), 0)
    feat_i = lax.broadcasted_iota(jnp.int32, (LANE, 2 * tq), 0)
    heads = []
    for hh in range(ATTN_HEADS_PER_STEP):
        cs = slice(hh * DA_V_DIM, (hh + 1) * DA_V_DIM)
        h = ATTN_HEADS_PER_STEP * hp + hh
        q = q_ref[cs, :].astype(F32)
        feat = jnp.zeros((LANE, 2 * tq), F32)
        for j in range(SLOPE_PIECES):
            feat = jnp.where(feat_i == j, slopes_ref[h, j], feat)
        q_t = jnp.concatenate(
            [jnp.concatenate([jnp.where(dim_i < DA_HEAD_DIM, q, 0.0),
                              jnp.where(dim_i >= DA_HEAD_DIM, q, 0.0)], axis=1), feat], axis=0)
        heads.append((cs, slopes_ref[h, SLOPE_PIECES], q_t.astype(BF16)))

    def scores(head, start):
        k_aug = jnp.concatenate([k_ref[pl.ds(start, tq), head[0]], pos], axis=1)
        s = jnp.dot(k_aug, head[2], preferred_element_type=F32)
        return s, jnp.max(s, axis=0, keepdims=True)

    def update(hh, m, s, s_max, c, pv_and_sum):
        m_new = jnp.maximum(m, s_max + c)
        alpha = jnp.exp2(m - m_new)
        p = jnp.exp2(s - (m_new - c))
        acc_ref[hh] = alpha * acc_ref[hh] + pv_and_sum(p)
        return m_new

    def tile_start(t):
        return pl.multiple_of(t * tq, tq)

    def full_tile(hh, m, s, s_max, start):
        c = heads[hh][1] * (start - row0).astype(F32)
        v_t = jnp.concatenate([v_ref[pl.ds(start, tq), heads[hh][0]].T, ones_rows], axis=0)
        return update(hh, m, s, s_max, c,
                      lambda p: jnp.dot(v_t, p.astype(BF16), preferred_element_type=F32))

    n_full = qi - 1
    trips = jnp.maximum(n_full - 1, 0) // 2
    carries = []
    for hh, head in enumerate(heads):
        acc_ref[hh] = jnp.zeros((ATTN_ACC_ROWS, 2 * tq), F32)
        s, s_max = scores(head, tile_start(1))
        sa_ref[hh] = s
        carries.append((jnp.full((1, 2 * tq), NEG, F32), s_max))

    def issue(buf_ref, t):
        maxes = []
        for hh, head in enumerate(heads):
            s, s_max = scores(head, tile_start(t))
            buf_ref[hh] = s
            maxes.append(s_max)
        return maxes

    def consume(buf_ref, t, ms, maxes, off=None):
        out = []
        for hh, (m, s_max) in enumerate(zip(ms, maxes)):
            s = buf_ref[hh]
            if off is not None:
                s, s_max = s + off, s_max + off
            out.append(full_tile(hh, m, s, s_max, tile_start(t)))
        return out

    def body(i, carries):
        t = 1 + 2 * i
        max_b = issue(sb_ref, t + 1)
        ms = consume(sa_ref, t, [c[0] for c in carries], [c[1] for c in carries])
        max_a = issue(sa_ref, t + 2)
        ms = consume(sb_ref, t + 1, ms, max_b)
        return tuple(zip(ms, max_a))

    carries = lax.fori_loop(0, trips, body, tuple(carries))

    t_a = 1 + 2 * trips
    remaining = n_full - 2 * trips
    ms = lax.cond(remaining >= 1,
                  lambda cs: tuple(consume(sa_ref, t_a, [c[0] for c in cs], [c[1] for c in cs])),
                  lambda cs: tuple(c[0] for c in cs), tuple(carries))

    def second_tile(ms):
        return tuple(consume(sb_ref, t_a + 1, list(ms), issue(sb_ref, t_a + 1)))

    ms = lax.cond(remaining >= 2, second_tile, lambda ms: ms, tuple(ms))

    half = tq // 2
    key_i = lax.broadcasted_iota(jnp.int32, (N_META + half, 2 * tq), 0) - N_META
    qry_i = lax.broadcasted_iota(jnp.int32, (N_META + half, 2 * tq), 1) & (tq - 1)
    meta_base = jnp.where(qi >= 1, PAD, -PAD)
    key_row = jnp.where(key_i < 0, meta_base + key_i, row0 + key_i)
    mask_1 = (key_row <= row0 + qry_i) & (key_row >= FIRST_KEY)
    key_2 = half + lax.broadcasted_iota(jnp.int32, (half, tq), 0)
    qry_2 = half + (lax.broadcasted_iota(jnp.int32, (half, tq), 1) & (half - 1))
    mask_2 = (key_2 <= qry_2) & (row0 + key_2 >= FIRST_KEY)
    pos_1 = jnp.concatenate([pos[:N_META], pos[:half]], axis=0)
    late = [slice(half, tq), slice(tq + half, 2 * tq)]
    lp = lam_ref[...]
    lam = (jnp.exp(jnp.sum(lp[0:1] * lp[1:2], axis=1, keepdims=True))
           - jnp.exp(jnp.sum(lp[2:3] * lp[3:4], axis=1, keepdims=True)) + lambda_init)
    row = row0 + lax.broadcasted_iota(jnp.int32, (tq, 1), 0)
    n_sum = ATTN_ACC_ROWS - DA_V_DIM

    def values_and_sums(v, p):
        return jnp.concatenate(
            [lax.dot_general(v, p.astype(BF16), TN_DIMS, preferred_element_type=F32),
             jnp.broadcast_to(jnp.sum(p, axis=0, keepdims=True), (n_sum, p.shape[1]))], axis=0)

    for hh, (head, m) in enumerate(zip(heads, ms)):
        cs, slope, q_t = head
        k_1 = jnp.concatenate([k_ref[pl.ds(FIRST_KEY, N_META), cs], k_ref[pl.ds(row0, half), cs]], axis=0)
        v_1 = jnp.concatenate([v_ref[pl.ds(FIRST_KEY, N_META), cs], v_ref[pl.ds(row0, half), cs]], axis=0)
        s_1 = jnp.dot(jnp.concatenate([k_1, pos_1], axis=1), q_t, preferred_element_type=F32)
        s_1 = s_1 + jnp.where(key_i[:, :1] < 0, slope * (FIRST_KEY - row0).astype(F32), 0.0)
        s_1 = jnp.where(mask_1, s_1, NEG)
        m = update(hh, m, s_1, jnp.max(s_1, axis=0, keepdims=True), 0.0,
                   functools.partial(values_and_sums, v_1))
        start_2 = pl.multiple_of(row0 + half, half)
        k_2 = jnp.concatenate([k_ref[pl.ds(start_2, half), cs], pos[half:]], axis=1)
        q_2 = jnp.concatenate([q_t[:, c] for c in late], axis=1)
        s_2 = jnp.where(mask_2, jnp.dot(k_2, q_2, preferred_element_type=F32), NEG)
        m_2 = jnp.concatenate([m[:, c] for c in late], axis=1)
        m_new = jnp.maximum(m_2, jnp.max(s_2, axis=0, keepdims=True))
        alpha = jnp.exp2(m_2 - m_new)
        pv_2 = values_and_sums(v_ref[pl.ds(start_2, half), cs], jnp.exp2(s_2 - m_new))
        for j, c in enumerate(late):
            jc = slice(j * half, (j + 1) * half)
            acc_ref[hh, :, c] = alpha[:, jc] * acc_ref[hh, :, c] + pv_2[:, jc]
        acc = acc_ref[hh]
        o_t = acc[:DA_V_DIM] * (1.0 / acc[DA_V_DIM:DA_V_DIM + 1])
        d = (o_t[:, :tq] - lam * o_t[:, tq:]).T
        d = jnp.where(row >= FIRST_KEY, d, 0.0)
        y = _rms_scale(d, sub_ref[...]) * (1.0 - lambda_init)
        o_ref[:, cs] = y.astype(o_ref.dtype)


def diff_attention(q_t, kv, lam_params, subln_w, lambda_init):
    B, Lp, _ = kv.shape
    hps = ATTN_HEADS_PER_STEP
    w = hps * DA_V_DIM
    n_col = DA_HEADS // hps
    n_tiles = Lp // ATTN_TQ
    slope = 2.0 ** (-8.0 * jnp.arange(1, DA_HEADS + 1, dtype=F32) / DA_HEADS) * LOG2E
    pieces, rest = [], slope
    for _ in range(SLOPE_PIECES):
        pieces.append(rest.astype(BF16).astype(F32))
        rest = rest - pieces[-1]
    slopes = jnp.stack(pieces + [slope], axis=1)
    kv_spec = lambda sec: pl.BlockSpec((None, Lp, w), lambda b, h, i: (b, 0, sec * n_col + h))
    return pl.pallas_call(
        functools.partial(_attn_body, n_tiles=n_tiles, lambda_init=lambda_init),
        grid=(B, n_col, n_tiles),
        in_specs=[pl.BlockSpec(memory_space=pltpu.SMEM),
                  pl.BlockSpec((w, ATTN_TQ), lambda b, h, i: (h, b * n_tiles + i)),
                  kv_spec(0),
                  kv_spec(1),
                  _resident((4, DA_HEAD_DIM)),
                  _resident((1, DA_V_DIM))],
        out_specs=pl.BlockSpec((None, ATTN_TQ, w), lambda b, h, i: (b, i, h)),
        out_shape=jax.ShapeDtypeStruct((B, Lp, DA_HEADS * DA_V_DIM), BF16),
        scratch_shapes=[pltpu.VMEM((hps, ATTN_TQ, 2 * ATTN_TQ), F32),
                        pltpu.VMEM((hps, ATTN_TQ, 2 * ATTN_TQ), F32),
                        pltpu.VMEM((hps, ATTN_ACC_ROWS, 2 * ATTN_TQ), F32)],
        compiler_params=pltpu.CompilerParams(
            dimension_semantics=("parallel", "parallel", "arbitrary"), vmem_limit_bytes=ATTN_VMEM_LIMIT),
        name="diff_attention",
    )(slopes, q_t, kv, kv, lam_params.astype(F32), subln_w.reshape(1, DA_V_DIM).astype(F32))


def _gla_tables():
    C = GLA_CHUNK
    i = np.arange(C)[:, None]
    t = np.arange(C)[None, :]
    blocks = [t <= i, t > i]
    level = np.full((C, C), GLA_LEVELS + 1, np.int32)
    level[np.arange(C), np.arange(C)] = GLA_LEVELS
    for l in range(GLA_LEVELS):
        s = C >> (l + 1)
        m = (i // (2 * s)) * 2 * s + s - 1
        upper = (i & s) != 0
        blocks.append(np.where(upper, (t > m) & (t <= i), (t > i) & (t <= m)))
        level[(i > t) & (((i ^ t) >> (GLA_LEVELS - 1 - l)) == 1)] = l
    sums = np.concatenate(blocks, axis=0).astype(np.float32)
    sums = np.concatenate([sums, sums], axis=1)
    return jnp.asarray(sums, BF16), jnp.asarray(level)


def _gla_body(level_ref, sums_ref, q_ref, k_ref, v_ref, g_ref, gl_ref, nw_ref, o_ref, st_ref, *, n_seqs):
    C = GLA_CHUNK

    @pl.when(pl.program_id(1) == 0)
    def _():
        st_ref[...] = jnp.zeros_like(st_ref)

    level = level_ref[...]
    at_level = [level == l for l in range(GLA_LEVELS + 1)]
    for cc in range(GLA_CHUNKS_PER_STEP):
        for seq in range(n_seqs):
            _gla_chunk(seq, slice(cc * C, (cc + 1) * C), at_level, sums_ref, q_ref, k_ref, v_ref, g_ref,
                       gl_ref, nw_ref, o_ref, st_ref)


def _gla_chunk(seq, rows, at_level, sums_ref, q_ref, k_ref, v_ref, g_ref, gl_ref, nw_ref, o_ref, st_ref):
    C = GLA_CHUNK
    gl = gl_ref[seq, rows, :]
    g_hi = gl.astype(BF16)
    g_lo = (gl - g_hi.astype(F32)).astype(BF16)
    w_all = jnp.dot(sums_ref[...], jnp.concatenate([g_hi, g_lo], axis=0),
                    preferred_element_type=F32)

    key_cols = [slice(h * GLA_HK, (h + 1) * GLA_HK) for h in range(GLA_HEADS)]
    qs = [q_ref[seq, rows, ks].astype(F32) for ks in key_cols]
    ks_ = [k_ref[seq, rows, ks].astype(F32) for ks in key_cols]
    ws = [w_all[:, ks] for ks in key_cols]
    scores = [jnp.where(at_level[GLA_LEVELS], jnp.sum(q * k, axis=1, keepdims=True), 0.0)
              for q, k in zip(qs, ks_)]
    zero = jnp.zeros((C, GLA_HK), BF16)
    for h0 in range(0, GLA_HEADS, 2):
        for l in range(GLA_LEVELS):
            e = [jnp.exp2(ws[h][(2 + l) * C:(3 + l) * C]) for h in (h0, h0 + 1)]
            qe2 = jnp.concatenate([(qs[h0] * e[0]).astype(BF16), (qs[h0 + 1] * e[1]).astype(BF16)], axis=1)
            ke2 = jnp.concatenate(
                [jnp.concatenate([(ks_[h0] * e[0]).astype(BF16), zero], axis=1),
                 jnp.concatenate([zero, (ks_[h0 + 1] * e[1]).astype(BF16)], axis=1)], axis=0)
            a_l = lax.dot_general(qe2, ke2, NT_DIMS, preferred_element_type=F32)
            scores[h0] = jnp.where(at_level[l], a_l[:, :C], scores[h0])
            scores[h0 + 1] = jnp.where(at_level[l], a_l[:, C:], scores[h0 + 1])

    for h in range(GLA_HEADS):
        vs = slice(h * GLA_HV, (h + 1) * GLA_HV)
        q, k, w, a = qs[h], ks_[h], ws[h], scores[h]
        v = v_ref[seq, rows, vs]
        cum = w[0:C]
        state_t = st_ref[seq, h]
        qe = (q * jnp.exp2(cum)).astype(BF16)
        o = (jnp.dot(a.astype(BF16), v, preferred_element_type=F32)
             + lax.dot_general(qe, state_t.astype(BF16), NT_DIMS, preferred_element_type=F32))
        kd = (k * jnp.exp2(w[C:2 * C])).astype(BF16)
        st_ref[seq, h] = (state_t * jnp.exp2(cum[C - 1:C])
                     + lax.dot_general(v, kd, TN_DIMS, preferred_element_type=F32))
        g = g_ref[seq, rows, vs].astype(F32)
        y = _rms_scale(o, nw_ref[...]) * (g / (1.0 + jnp.exp(-g)))
        o_ref[seq, rows, vs] = y.astype(o_ref.dtype)


def gla_mixer(qkvg, glog, norm_w):
    B, Lp, _ = qkvg.shape
    C = GLA_CHUNK
    sums, level = _gla_tables()
    kw, vw = GLA_KEY_DIM, GLA_VAL_DIM
    R = GLA_CHUNKS_PER_STEP * C
    S = GLA_SEQS_PER_STEP if B % GLA_SEQS_PER_STEP == 0 else 1
    return pl.pallas_call(
        functools.partial(_gla_body, n_seqs=S),
        grid=(B // S, Lp // R),
        in_specs=[_resident((C, C)),
                  _resident(((2 + GLA_LEVELS) * C, 2 * C)),
                  pl.BlockSpec((S, R, kw), lambda b, c: (b, c, 0)),
                  pl.BlockSpec((S, R, kw), lambda b, c: (b, c, 1)),
                  pl.BlockSpec((S, R, vw), lambda b, c: (b, c, 1)),
                  pl.BlockSpec((S, R, vw), lambda b, c: (b, c, 2)),
                  pl.BlockSpec((S, R, kw), lambda b, c: (b, c, 0)),
                  _resident((1, GLA_HV))],
        out_specs=pl.BlockSpec((S, R, vw), lambda b, c: (b, c, 0)),
        out_shape=jax.ShapeDtypeStruct((B, Lp, vw), BF16),
        scratch_shapes=[pltpu.VMEM((S, GLA_HEADS, GLA_HV, GLA_HK), F32)],
        compiler_params=pltpu.CompilerParams(
            dimension_semantics=("parallel", "arbitrary"), vmem_limit_bytes=VMEM_LIMIT),
        name="gla_mixer",
    )(level, sums, qkvg, qkvg, qkvg, qkvg, glog, norm_w.reshape(1, GLA_HV).astype(F32))


def _mlp_rows(h, o, wo_ref, nw_ref, wu_ref, wd_ref, out_ref, f_chunk):
    h1 = h + jnp.dot(o, wo_ref[...], preferred_element_type=F32)
    hn = _rms_scale(h1, nw_ref[...]).astype(BF16)
    out_ref[...] = h1
    for c in range(0, wu_ref.shape[1], f_chunk):
        u = jnp.dot(hn, wu_ref[:, c:c + f_chunk], preferred_element_type=F32)
        u = jnp.square(jnp.maximum(u, 0.0)).astype(BF16)
        out_ref[...] += jnp.dot(u, wd_ref[c:c + f_chunk, :], preferred_element_type=F32)


def _out_mlp_body(h_ref, o_ref, wo_ref, nw_ref, wu_ref, wd_ref, out_ref, *, f_chunk):
    _mlp_rows(h_ref[...], o_ref[...], wo_ref, nw_ref, wu_ref, wd_ref, out_ref, f_chunk)


def _out_mlp_final_body(ha_ref, hb_ref, oa_ref, ob_ref, wo_ref, nw_ref, wu_ref, wd_ref, fw_ref, out_ref, *,
                        f_chunk):
    h = jnp.concatenate([ha_ref[...], hb_ref[...]], axis=0)
    o = jnp.concatenate([oa_ref[...], ob_ref[...]], axis=0)
    _mlp_rows(h, o, wo_ref, nw_ref, wu_ref, wd_ref, out_ref, f_chunk)
    out_ref[...] = _rms_scale(out_ref[...], fw_ref[...])


def out_mlp(h, o, w_out, norm_w, w_up, w_down):
    T, D = h.shape
    F = w_up.shape[1]
    tm = _row_tile(T)
    return pl.pallas_call(
        functools.partial(_out_mlp_body, f_chunk=MLP_F_CHUNK),
        grid=(T // tm,),
        in_specs=[pl.BlockSpec((tm, D), lambda i: (i, 0)),
                  pl.BlockSpec((tm, D), lambda i: (i, 0)),
                  _resident((D, D)),
                  _resident((1, D)),
                  _resident((D, F)),
                  _resident((F, D))],
        out_specs=pl.BlockSpec((tm, D), lambda i: (i, 0)),
        out_shape=jax.ShapeDtypeStruct((T, D), F32),
        compiler_params=pltpu.CompilerParams(
            dimension_semantics=("parallel",), vmem_limit_bytes=VMEM_LIMIT),
        name="out_mlp",
    )(h, o, w_out, norm_w.reshape(1, D), w_up, w_down)


def out_mlp_final(h, o, w_out, norm_w, w_up, w_down, final_w):
    B, Lp, D = h.shape
    F = w_up.shape[1]
    t = PAD
    lead_blocks = PAD // t
    first = pl.BlockSpec((None, t, D), lambda b, i: (b, 2 * i + lead_blocks, 0))
    second = pl.BlockSpec((None, t, D), lambda b, i: (b, 2 * i + lead_blocks + 1, 0))
    return pl.pallas_call(
        functools.partial(_out_mlp_final_body, f_chunk=MLP_F_CHUNK),
        grid=(B, (Lp - PAD) // (2 * t)),
        in_specs=[first, second, first, second,
                  _resident((D, D)),
                  _resident((1, D)),
                  _resident((D, F)),
                  _resident((F, D)),
                  _resident((1, D))],
        out_specs=pl.BlockSpec((None, 2 * t, D), lambda b, i: (b, i, 0)),
        out_shape=jax.ShapeDtypeStruct((B, Lp - PAD, D), F32),
        compiler_params=pltpu.CompilerParams(
            dimension_semantics=("parallel", "parallel"), vmem_limit_bytes=VMEM_LIMIT),
        name="out_mlp_final",
    )(h, h, o, o, w_out, norm_w.reshape(1, D), w_up, w_down, final_w.reshape(1, D))


def _lambda_init_for(layer_idx):
    return 0.8 - 0.6 * math.exp(-0.3 * layer_idx)


def kernel(x, meta_tokens, mix_norm_w, attn_w_in, attn_lambda, attn_subln_w, attn_w_out, gla_w_in, gla_w_gate_up, gla_gate_bias, gla_norm_w, gla_w_out, mlp_norm_w, mlp_w_up, mlp_w_down, final_norm_w):
    B, seq, D = x.shape
    Lp = PAD + seq
    T = B * Lp
    lead = jnp.concatenate([jnp.zeros((FIRST_KEY, D), x.dtype), meta_tokens.astype(x.dtype)], axis=0)

    for i in range(DEPTH):
        j = i // 2
        if i % 2 == 0:
            col_scale = jnp.where(jnp.arange(3 * D) < D, ATTN_Q_SCALE, 1.0).astype(F32)
            w_in = (attn_w_in[j] * col_scale).astype(BF16)
            if i == 0:
                h, q_t, kv = norm_proj_first(x, lead, mix_norm_w[i], w_in)
            else:
                q_t, kv = norm_proj(h, mix_norm_w[i], w_in)
            o = diff_attention(q_t, kv.reshape(B, Lp, 2 * D), attn_lambda[j], attn_subln_w[j],
                               _lambda_init_for(i))
            w_out = attn_w_out[j]
        else:
            n_main = 2 * GLA_KEY_DIM + 2 * GLA_VAL_DIM
            w_in = gla_w_in[j]
            w_gz = jnp.pad(w_in[:, n_main:], ((0, 0), (0, LANE - GLA_GATE_RANK))).astype(BF16)
            w_gu = jnp.pad(gla_w_gate_up[j], ((0, LANE - GLA_GATE_RANK), (0, 0))).astype(BF16)
            q_scale = jnp.where(jnp.arange(n_main) < GLA_KEY_DIM, GLA_HK ** -0.5, 1.0).astype(F32)
            qkvg, glog = gla_proj(h, mix_norm_w[i], (w_in[:, :n_main] * q_scale).astype(BF16), w_gz, w_gu,
                                  gla_gate_bias[j])
            o = gla_mixer(qkvg.reshape(B, Lp, n_main), glog.reshape(B, Lp, GLA_KEY_DIM), gla_norm_w[j])
            w_out = gla_w_out[j]
        mlp_args = (w_out.astype(BF16), mlp_norm_w[i], mlp_w_up[i].astype(BF16), mlp_w_down[i].astype(BF16))
        if i < DEPTH - 1:
            h = out_mlp(h, o.reshape(T, D), *mlp_args)
    return out_mlp_final(h.reshape(B, Lp, D), o.reshape(B, Lp, D), *mlp_args, final_norm_w)
```
